```python
import math
import jax
import jax.numpy as jnp
from jax import lax
import numpy as np


D_MODEL = 1024
BATCH = 8
SEQ = 2048
DEPTH = 4

GRID_W = 64
CTX_LEN = 256
EPS = 1e-6
CHUNK = 128
ATTN_BLOCK = 128
D_FF = 4 * D_MODEL
CONV_K = 5

MLA_HEADS = 8
MLA_NOPE = 64
MLA_ROPE = 32
MLA_V = 64
Q_LORA = 256
KV_LORA = 256
ROPE_THETA = 10000.0
MLA_SCALE = (MLA_NOPE + MLA_ROPE) ** -0.5

SSM_HEADS = 8
SSM_HEADDIM = 64
SSM_DINNER = SSM_HEADS * SSM_HEADDIM
SSM_GROUPS = 2
SSM_STATE = 128

MLSTM_HEADS = 8
MLSTM_DQK = 64
MLSTM_DV = 128
MLSTM_QK = MLSTM_HEADS * MLSTM_DQK
MLSTM_VW = MLSTM_HEADS * MLSTM_DV

EVEN_IN = Q_LORA + KV_LORA + MLA_ROPE + SSM_DINNER + SSM_DINNER + 2 * SSM_GROUPS * SSM_STATE + 2 * SSM_HEADS
ODD_IN = 2 * MLSTM_QK + 2 * MLSTM_VW + 4 * MLSTM_HEADS

kernel_name = 'hybrid_mla_ssd_mlstm_prefix_dit'


def split_cols(u, sizes):
    offs = np.cumsum(sizes)[:-1].tolist()
    return jnp.split(u, offs, axis=-1)


def rmsnorm(x, w):
    xf = x.astype(jnp.float32)
    y = xf * lax.rsqrt(jnp.mean(xf * xf, axis=-1, keepdims=True) + EPS)
    return (y * w).astype(x.dtype)


def modulate(h, shift, scale):
    return h * (1.0 + scale) + shift


def sqrelu_mlp(h, w1, w2):
    return (jnp.square(jax.nn.relu(h @ w1)) @ w2).astype(h.dtype)


def dwconv_silu(x, w, b):
    ch = x.shape[-1]
    y = lax.conv_general_dilated(x, w[:, None, :].astype(x.dtype), window_strides=(1,),
                                 padding=[(CONV_K // 2, CONV_K // 2)],
                                 dimension_numbers=('NWC', 'WIO', 'NWC'), feature_group_count=ch)
    return jax.nn.silu(y + b)


def rope2d_tables(length):
    n_rows = length // GRID_W
    row = jnp.repeat(jnp.arange(n_rows), GRID_W).astype(jnp.float32)
    col = jnp.tile(jnp.arange(GRID_W), n_rows).astype(jnp.float32)
    half = MLA_ROPE // 2
    inv = 1.0 / (ROPE_THETA ** (jnp.arange(0, half, 2, dtype=jnp.float32) / half))
    ar = row[:, None] * inv
    ac = col[:, None] * inv
    return tuple(t[None, :, None, :] for t in (jnp.cos(ar), jnp.sin(ar), jnp.cos(ac), jnp.sin(ac)))


def rope_rotate(x, cos, sin):
    x1, x2 = jnp.split(x, 2, axis=-1)
    return jnp.concatenate([x1 * cos - x2 * sin, x2 * cos + x1 * sin], axis=-1)


def rope2d(x, tabs):
    cr, sr, cc, sc = tabs
    xr, xc = jnp.split(x, 2, axis=-1)
    return jnp.concatenate([rope_rotate(xr, cr, sr), rope_rotate(xc, cc, sc)], axis=-1).astype(x.dtype)


def attend(q, k, v):
    s = jnp.einsum('bqhd,bkhd->bhqk', q, k).astype(jnp.float32) * MLA_SCALE
    p = jax.nn.softmax(s, axis=-1).astype(v.dtype)
    return jnp.einsum('bhqk,bkhd->bqhd', p, v)


def ssd_scan(x, dt, a_coef, bm, cm, h0):
    bsz, length, nh, hp = x.shape
    ng, ns = bm.shape[-2:]
    nr = nh // ng
    nc = length // CHUNK
    xf = x.astype(jnp.float32).reshape(bsz, nc, CHUNK, ng, nr, hp)
    dtc = dt.reshape(bsz, nc, CHUNK, ng, nr)
    bc = bm.astype(jnp.float32).reshape(bsz, nc, CHUNK, ng, ns)
    cc = cm.astype(jnp.float32).reshape(bsz, nc, CHUNK, ng, ns)
    acum = jnp.cumsum(dtc * a_coef.reshape(ng, nr), axis=2)
    causal = jnp.tril(jnp.ones((CHUNK, CHUNK), bool))
    seg = acum[:, :, :, None] - acum[:, :, None, :]
    decay = jnp.exp(jnp.where(causal[:, :, None, None], seg, -jnp.inf))
    cb = jnp.einsum('bctgn,bcsgn->bctsg', cc, bc)
    w = cb[..., None] * decay * dtc[:, :, None]
    y = jnp.einsum('bctsgr,bcsgrp->bctgrp', w, xf)
    to_end = jnp.exp(acum[:, :, -1:] - acum) * dtc
    s_chunk = jnp.einsum('bcsgr,bcsgrp,bcsgn->bcgrpn', to_end, xf, bc)
    chunk_decay = jnp.exp(acum[:, :, -1])

    def step(h, inp):
        s_c, d_c = inp
        return d_c[..., None, None] * h + s_c, h

    h_last, h_in = lax.scan(step, h0.reshape(bsz, ng, nr, hp, ns),
                            (jnp.moveaxis(s_chunk, 1, 0), jnp.moveaxis(chunk_decay, 1, 0)))
    h_in = jnp.moveaxis(h_in, 0, 1)
    y = y + jnp.einsum('bctgn,bcgrpn->bctgrp', cc, h_in) * jnp.exp(acum)[..., None]
    return y.reshape(bsz, length, nh, hp), h_last.reshape(bsz, nh, hp, ns)


def mlstm_scan(q, k, v, i_pre, f_pre, state, with_output):
    bsz, length, nh, dk = q.shape
    dv = v.shape[-1]
    nc = length // CHUNK
    qc = q.astype(jnp.float32).reshape(bsz, nc, CHUNK, nh, dk)
    kc = k.astype(jnp.float32).reshape(bsz, nc, CHUNK, nh, dk)
    vc = v.astype(jnp.float32).reshape(bsz, nc, CHUNK, nh, dv)
    ic = i_pre.reshape(bsz, nc, CHUNK, nh)
    b = jnp.cumsum(jax.nn.log_sigmoid(f_pre).reshape(bsz, nc, CHUNK, nh), axis=2)
    b_last = b[:, :, -1]
    w_end = b_last[:, :, None] - b + ic

    def step(carry, inp):
        c_st, n_st, m_st = carry
        k_c, v_c, w_c, bl_c = inp
        m_new = jnp.maximum(bl_c + m_st, jnp.max(w_c, axis=1))
        keep = jnp.exp(bl_c + m_st - m_new)
        wt = jnp.exp(w_c - m_new[:, None])
        c_new = keep[..., None, None] * c_st + jnp.einsum('bsh,bshv,bshk->bhvk', wt, v_c, k_c)
        n_new = keep[..., None] * n_st + jnp.einsum('bsh,bshk->bhk', wt, k_c)
        return (c_new, n_new, m_new), (c_st, n_st, m_st)

    xs = (jnp.moveaxis(kc, 1, 0), jnp.moveaxis(vc, 1, 0), jnp.moveaxis(w_end, 1, 0), jnp.moveaxis(b_last, 1, 0))
    final, (c_in, n_in, m_in) = lax.scan(step, state, xs)
    if not with_output:
        return None, final
    c_in = jnp.moveaxis(c_in, 0, 1)
    n_in = jnp.moveaxis(n_in, 0, 1)
    m_in = jnp.moveaxis(m_in, 0, 1)
    causal = jnp.tril(jnp.ones((CHUNK, CHUNK), bool))
    logd = b[:, :, :, None] - b[:, :, None, :] + ic[:, :, None, :]
    logd = jnp.where(causal[:, :, None], logd, -jnp.inf)
    g = b + m_in[:, :, None]
    m_t = jnp.maximum(g, jnp.max(logd, axis=3))
    dmat = jnp.exp(logd - m_t[:, :, :, None])
    keep = jnp.exp(g - m_t)
    a = jnp.einsum('bcthd,bcshd->bctsh', qc, kc) * dmat
    num = jnp.einsum('bctsh,bcshv->bcthv', a, vc) + keep[..., None] * jnp.einsum('bcthk,bchvk->bcthv', qc, c_in)
    den = jnp.sum(a, axis=3) + keep * jnp.einsum('bcthk,bchk->bcth', qc, n_in)
    h = num / jnp.maximum(jnp.abs(den), jnp.exp(-m_t))[..., None]
    return h.reshape(bsz, length, nh, dv), final


def gated_rmsnorm(y, z, w):
    b_, l_ = z.shape[:2]
    g = y.reshape(b_, l_, SSM_GROUPS, -1) * jax.nn.silu(z.astype(jnp.float32)).reshape(b_, l_, SSM_GROUPS, -1)
    g = g * lax.rsqrt(jnp.mean(g * g, axis=-1, keepdims=True) + EPS)
    return g.reshape(b_, l_, SSM_DINNER) * w


def mla_ssd_mixer(h_ctx, h_lat, rope, w_in, q_norm_w, w_uq, kv_norm_w, w_ukv, conv_w, conv_b,
                  dt_bias, a_log, d_skip, ssm_norm_w, w_out, need_ctx):
    sizes = [Q_LORA, KV_LORA, MLA_ROPE, SSM_DINNER, SSM_DINNER + 2 * SSM_GROUPS * SSM_STATE, 2 * SSM_HEADS]
    parts_c = split_cols(h_ctx @ w_in, sizes)
    parts_l = split_cols(h_lat @ w_in, sizes)

    def mla_qkv(cq, ckv, k_rope, tabs):
        b_, l_ = cq.shape[:2]
        q = (rmsnorm(cq, q_norm_w) @ w_uq).reshape(b_, l_, MLA_HEADS, MLA_NOPE + MLA_ROPE)
        kv = (rmsnorm(ckv, kv_norm_w) @ w_ukv).reshape(b_, l_, MLA_HEADS, MLA_NOPE + MLA_V)
        q_nope, q_rope = jnp.split(q, [MLA_NOPE], axis=-1)
        k_nope, v = jnp.split(kv, [MLA_NOPE], axis=-1)
        k_rope = k_rope[:, :, None, :]
        if tabs is not None:
            q_rope = rope2d(q_rope, tabs)
            k_rope = rope2d(k_rope, tabs)
        q = jnp.concatenate([q_nope, q_rope], axis=-1)
        k = jnp.concatenate([k_nope, jnp.broadcast_to(k_rope, k_nope.shape[:3] + (MLA_ROPE,)).astype(k_nope.dtype)], axis=-1)
        return q, k, v

    q_c, k_c, v_c = mla_qkv(parts_c[0], parts_c[1], parts_c[2], None)
    q_l, k_l, v_l = mla_qkv(parts_l[0], parts_l[1], parts_l[2], rope)
    k_all = jnp.concatenate([k_l, k_c], axis=1)
    v_all = jnp.concatenate([v_l, v_c], axis=1)
    b_, l_ = h_lat.shape[:2]
    nb = l_ // ATTN_BLOCK
    q_blocks = jnp.moveaxis(q_l.reshape(b_, nb, ATTN_BLOCK, MLA_HEADS, MLA_NOPE + MLA_ROPE), 1, 0)
    o_l = lax.map(lambda qb: attend(qb, k_all, v_all), q_blocks)
    o_l = jnp.moveaxis(o_l, 0, 1).reshape(b_, l_, MLA_HEADS * MLA_V)

    def ssd_in(parts):
        z, xbc, dtr = parts[3], parts[4], parts[5]
        bb, ll = z.shape[:2]
        xbc = dwconv_silu(xbc, conv_w, conv_b)
        xs, bm, cm = split_cols(xbc, [SSM_DINNER, SSM_GROUPS * SSM_STATE, SSM_GROUPS * SSM_STATE])
        dt = jax.nn.softplus(dtr.astype(jnp.float32).reshape(bb, ll, 2, SSM_HEADS) + dt_bias)
        return (z, xs.reshape(bb, ll, SSM_HEADS, SSM_HEADDIM), bm.reshape(bb, ll, SSM_GROUPS, SSM_STATE),
                cm.reshape(bb, ll, SSM_GROUPS, SSM_STATE), dt)

    z_c, x_c, b_c, c_c, dt_c = ssd_in(parts_c)
    z_l, x_l, b_l, c_l, dt_l = ssd_in(parts_l)
    a_coef = -jnp.exp(a_log.astype(jnp.float32))
    h0 = jnp.zeros((b_, SSM_HEADS, SSM_HEADDIM, SSM_STATE), jnp.float32)
    y_c = d_skip[:, None] * x_c.astype(jnp.float32)
    y_l = d_skip[:, None] * x_l.astype(jnp.float32)
    for d in range(2):
        fl = (lambda t: jnp.flip(t, axis=1)) if d else (lambda t: t)
        yc_d, h_d = ssd_scan(fl(x_c), fl(dt_c[:, :, d]), a_coef[d], fl(b_c), fl(c_c), h0)
        yl_d, _ = ssd_scan(fl(x_l), fl(dt_l[:, :, d]), a_coef[d], fl(b_l), fl(c_l), h_d)
        y_c = y_c + fl(yc_d)
        y_l = y_l + fl(yl_d)
    s_l = gated_rmsnorm(y_l, z_l, ssm_norm_w).astype(o_l.dtype)
    out_l = (jnp.concatenate([o_l, s_l], axis=-1) @ w_out).astype(h_lat.dtype)
    if not need_ctx:
        return None, out_l
    o_c = attend(q_c, k_c, v_c).reshape(b_, q_c.shape[1], MLA_HEADS * MLA_V)
    s_c = gated_rmsnorm(y_c, z_c, ssm_norm_w).astype(o_c.dtype)
    out_c = (jnp.concatenate([o_c, s_c], axis=-1) @ w_out).astype(h_ctx.dtype)
    return out_c, out_l


def mlstm_mixer(h_ctx, h_lat, w_in, conv_w, conv_b, i_bias, f_bias, head_norm_w, w_out, need_ctx):
    def project(h):
        bb, ll = h.shape[:2]
        qk, v, o, ig, fg = split_cols(h @ w_in, [2 * MLSTM_QK, MLSTM_VW, MLSTM_VW, 2 * MLSTM_HEADS, 2 * MLSTM_HEADS])
        q, k = jnp.split(dwconv_silu(qk, conv_w, conv_b), 2, axis=-1)
        q = q.reshape(bb, ll, MLSTM_HEADS, MLSTM_DQK)
        k = k.reshape(bb, ll, MLSTM_HEADS, MLSTM_DQK) * (MLSTM_DQK ** -0.5)
        v = v.reshape(bb, ll, MLSTM_HEADS, MLSTM_DV)
        ig = ig.astype(jnp.float32).reshape(bb, ll, 2, MLSTM_HEADS) + i_bias
        fg = fg.astype(jnp.float32).reshape(bb, ll, 2, MLSTM_HEADS) + f_bias
        return q, k, v, o, ig, fg

    q_c, k_c, v_c, o_c, i_c, f_c = project(h_ctx)
    q_l, k_l, v_l, o_l, i_l, f_l = project(h_lat)
    b_ = h_lat.shape[0]
    zero = (jnp.zeros((b_, MLSTM_HEADS, MLSTM_DV, MLSTM_DQK), jnp.float32),
            jnp.zeros((b_, MLSTM_HEADS, MLSTM_DQK), jnp.float32),
            jnp.zeros((b_, MLSTM_HEADS), jnp.float32))
    lat_h = []
    ctx_h = []
    for d in range(2):
        fl = (lambda t: jnp.flip(t, axis=1)) if d else (lambda t: t)
        hc_d, st = mlstm_scan(fl(q_c), fl(k_c), fl(v_c), fl(i_c[:, :, d]), fl(f_c[:, :, d]), zero, need_ctx)
        hl_d, _ = mlstm_scan(fl(q_l), fl(k_l), fl(v_l), fl(i_l[:, :, d]), fl(f_l[:, :, d]), st, True)
        lat_h.append(fl(hl_d))
        if need_ctx:
            ctx_h.append(fl(hc_d))

    def finish(ht, o, like):
        bb, ll = ht.shape[:2]
        hn = ht * lax.rsqrt(jnp.mean(ht * ht, axis=-1, keepdims=True) + EPS)
        hn = hn.reshape(bb, ll, MLSTM_VW) * head_norm_w
        return ((jax.nn.sigmoid(o.astype(jnp.float32)) * hn) @ w_out).astype(like.dtype)

    out_l = finish(lat_h[0] + lat_h[1], o_l, h_lat)
    if not need_ctx:
        return None, out_l
    return finish(ctx_h[0] + ctx_h[1], o_c, h_ctx), out_l


def setup_inputs(seed: int = 0) -> dict:
    key = jax.random.key(seed)
    ks = iter(jax.random.split(key, 48))
    f32 = jnp.float32
    ne = (DEPTH + 1) // 2
    no = DEPTH // 2

    def nrm(shape, scale):
        return jax.random.normal(next(ks), shape, f32) * scale

    def gain(shape):
        return 1.0 + nrm(shape, 0.02)

    x = nrm((BATCH, SEQ, D_MODEL), 1.0)
    c = nrm((BATCH, D_MODEL), 1.0)
    ctx = nrm((BATCH, CTX_LEN, D_MODEL), 1.0)
    c_ctx = nrm((D_MODEL,), 1.0)
    ada_w = nrm((DEPTH, D_MODEL, 6 * D_MODEL), 0.5 * D_MODEL ** -0.5)
    ada_b = nrm((DEPTH, 6 * D_MODEL), 0.02)
    norm_mix_w = gain((DEPTH, D_MODEL))
    norm_mlp_w = gain((DEPTH, D_MODEL))
    mlp_w1 = nrm((DEPTH, D_MODEL, D_FF), D_MODEL ** -0.5)
    mlp_w2 = nrm((DEPTH, D_FF, D_MODEL), D_FF ** -0.5)
    ev_w_in = nrm((ne, D_MODEL, EVEN_IN), D_MODEL ** -0.5)
    ev_q_norm_w = gain((ne, Q_LORA))
    ev_w_uq = nrm((ne, Q_LORA, MLA_HEADS * (MLA_NOPE + MLA_ROPE)), Q_LORA ** -0.5)
    ev_kv_norm_w = gain((ne, KV_LORA))
    ev_w_ukv = nrm((ne, KV_LORA, MLA_HEADS * (MLA_NOPE + MLA_V)), KV_LORA ** -0.5)
    ev_conv_w = nrm((ne, CONV_K, SSM_DINNER + 2 * SSM_GROUPS * SSM_STATE), CONV_K ** -0.5)
    ev_conv_b = nrm((ne, SSM_DINNER + 2 * SSM_GROUPS * SSM_STATE), 0.02)
    dt0 = jnp.exp(jax.random.uniform(next(ks), (ne, 2, SSM_HEADS), f32, math.log(1e-3), math.log(1e-1)))
    ev_dt_bias = dt0 + jnp.log(-jnp.expm1(-dt0))
    ev_a_log = jnp.log(jax.random.uniform(next(ks), (ne, 2, SSM_HEADS), f32, 1.0, 16.0))
    ev_d_skip = gain((ne, SSM_HEADS))
    ev_ssm_norm_w = gain((ne, SSM_DINNER))
    ev_w_out = nrm((ne, MLA_HEADS * MLA_V + SSM_DINNER, D_MODEL), (MLA_HEADS * MLA_V + SSM_DINNER) ** -0.5)
    od_w_in = nrm((no, D_MODEL, ODD_IN), D_MODEL ** -0.5)
    od_conv_w = nrm((no, CONV_K, 2 * MLSTM_QK), CONV_K ** -0.5)
    od_conv_b = nrm((no, 2 * MLSTM_QK), 0.02)
    od_i_bias = nrm((no, 2, MLSTM_HEADS), 0.1)
    od_f_bias = jnp.linspace(3.0, 6.0, MLSTM_HEADS, dtype=f32) + nrm((no, 2, MLSTM_HEADS), 0.1)
    od_head_norm_w = gain((no, MLSTM_VW))
    od_w_out = nrm((no, MLSTM_VW, D_MODEL), MLSTM_VW ** -0.5)
    final_norm_w = gain((D_MODEL,))
    return {'x': x, 'c': c, 'ctx': ctx, 'c_ctx': c_ctx, 'ada_w': ada_w, 'ada_b': ada_b,
            'norm_mix_w': norm_mix_w, 'norm_mlp_w': norm_mlp_w, 'mlp_w1': mlp_w1, 'mlp_w2': mlp_w2,
            'ev_w_in': ev_w_in, 'ev_q_norm_w': ev_q_norm_w, 'ev_w_uq': ev_w_uq, 'ev_kv_norm_w': ev_kv_norm_w,
            'ev_w_ukv': ev_w_ukv, 'ev_conv_w': ev_conv_w, 'ev_conv_b': ev_conv_b, 'ev_dt_bias': ev_dt_bias,
            'ev_a_log': ev_a_log, 'ev_d_skip': ev_d_skip, 'ev_ssm_norm_w': ev_ssm_norm_w, 'ev_w_out': ev_w_out,
            'od_w_in': od_w_in, 'od_conv_w': od_conv_w, 'od_conv_b': od_conv_b, 'od_i_bias': od_i_bias,
            'od_f_bias': od_f_bias, 'od_head_norm_w': od_head_norm_w, 'od_w_out': od_w_out,
            'final_norm_w': final_norm_w}


def reference(x, c, ctx, c_ctx, ada_w, ada_b, norm_mix_w, norm_mlp_w, mlp_w1, mlp_w2,
              ev_w_in, ev_q_norm_w, ev_w_uq, ev_kv_norm_w, ev_w_ukv, ev_conv_w, ev_conv_b,
              ev_dt_bias, ev_a_log, ev_d_skip, ev_ssm_norm_w, ev_w_out,
              od_w_in, od_conv_w, od_conv_b, od_i_bias, od_f_bias, od_head_norm_w, od_w_out,
              final_norm_w):
    rope = rope2d_tables(x.shape[1])
    s_lat = jax.nn.silu(c)
    s_ctx = jax.nn.silu(c_ctx)
    for layer in range(DEPTH):
        last = layer == DEPTH - 1
        sh1, sc1, g1, sh2, sc2, g2 = jnp.split(s_lat @ ada_w[layer] + ada_b[layer], 6, axis=-1)
        csh1, csc1, cg1, csh2, csc2, cg2 = jnp.split(s_ctx @ ada_w[layer] + ada_b[layer], 6, axis=-1)
        h_lat = modulate(rmsnorm(x, norm_mix_w[layer]), sh1[:, None], sc1[:, None])
        h_ctx = modulate(rmsnorm(ctx, norm_mix_w[layer]), csh1, csc1)
        if layer % 2 == 0:
            e = layer // 2
            o_ctx, o_lat = mla_ssd_mixer(h_ctx, h_lat, rope, ev_w_in[e], ev_q_norm_w[e], ev_w_uq[e],
                                         ev_kv_norm_w[e], ev_w_ukv[e], ev_conv_w[e], ev_conv_b[e],
                                         ev_dt_bias[e], ev_a_log[e], ev_d_skip[e], ev_ssm_norm_w[e],
                                         ev_w_out[e], not last)
        else:
            o = layer // 2
            o_ctx, o_lat = mlstm_mixer(h_ctx, h_lat, od_w_in[o], od_conv_w[o], od_conv_b[o], od_i_bias[o],
                                       od_f_bias[o], od_head_norm_w[o], od_w_out[o], not last)
        x = x + g1[:, None] * o_lat
        x = x + g2[:, None] * sqrelu_mlp(modulate(rmsnorm(x, norm_mlp_w[layer]), sh2[:, None], sc2[:, None]),
                                         mlp_w1[layer], mlp_w2[layer])
        if not last:
            ctx = ctx + cg1 * o_ctx
            ctx = ctx + cg2 * sqrelu_mlp(modulate(rmsnorm(ctx, norm_mlp_w[layer]), csh2, csc2),
                                         mlp_w1[layer], mlp_w2[layer])
    return rmsnorm(x, final_norm_w)
```

```python
import functools

import jax
import jax.numpy as jnp
import numpy as np
from jax import lax
from jax.experimental import pallas as pl
from jax.experimental.pallas import tpu as pltpu

F32 = jnp.float32
BF16 = jnp.bfloat16

EPS = 1e-6
GRID_W = 64
CHUNK = 128
CONV_K = 5
LANES = 128
SUBLANES = 8

MLA_HEADS = 8
MLA_NOPE = 64
MLA_ROPE = 32
MLA_V = 64
Q_LORA = 256
KV_LORA = 256
ROPE_THETA = 10000.0
MLA_SCALE = (MLA_NOPE + MLA_ROPE) ** -0.5
HEAD_PAD = 128

SSM_HEADS = 8
SSM_HEADDIM = 64
SSM_DINNER = SSM_HEADS * SSM_HEADDIM
SSM_GROUPS = 2
SSM_STATE = 128
SSM_XBC = SSM_DINNER + 2 * SSM_GROUPS * SSM_STATE

MLSTM_HEADS = 8
MLSTM_DQK = 64
MLSTM_DV = 128
MLSTM_QK = MLSTM_HEADS * MLSTM_DQK
MLSTM_VW = MLSTM_HEADS * MLSTM_DV

VMEM_LIMIT = 56 * 1024 * 1024

EV_CQ, EV_CKV, EV_Z, EV_XBC = 0, 256, 512, 1024
EV_ROPE_A, EV_ROPE_B, EV_DT, EV_N = 2048, 2176, 2304, 2432
OD_QK, OD_V, OD_O, OD_G, OD_N = 0, 1024, 2048, 3072, 3200


def _cparams(sem):
    return pltpu.CompilerParams(dimension_semantics=sem, vmem_limit_bytes=VMEM_LIMIT)


def _dot(a, b):
    return jnp.dot(a, b, preferred_element_type=F32)


def _dot_nt(a, b):
    return lax.dot_general(a, b, (((1,), (1,)), ((), ())), preferred_element_type=F32)


def _dot_tn(a, b):
    return lax.dot_general(a, b, (((0,), (0,)), ((), ())), preferred_element_type=F32)


def _sigmoid(x):
    return 1.0 / (1.0 + jnp.exp(-x))


def _softplus(x):
    return jnp.maximum(x, 0.0) + jnp.log(1.0 + jnp.exp(-jnp.abs(x)))


def _rms(x, w):
    ms = jnp.mean(x * x, axis=-1, keepdims=True)
    return x * lax.rsqrt(ms + EPS) * w


def _dot_f32_rows(x, m_bf16):
    hi = x.astype(BF16)
    r1 = x - hi.astype(F32)
    mid = r1.astype(BF16)
    lo = (r1 - mid.astype(F32)).astype(BF16)
    return _dot(hi, m_bf16) + _dot(mid, m_bf16) + _dot(lo, m_bf16)


def _ada_kernel(s_ref, w_ref, b_ref, o_ref):
    s = s_ref[...]
    s = (s * _sigmoid(s)).astype(BF16)
    o_ref[...] = _dot(s, w_ref[...].astype(BF16)) + b_ref[...]


def _ada(svec, ada_w, ada_b):
    depth, d, n6 = ada_w.shape
    rows = svec.shape[0]
    tn = 1024
    return pl.pallas_call(
        _ada_kernel,
        grid=(depth, n6 // tn),
        in_specs=[pl.BlockSpec((rows, d), lambda l, j: (0, 0)),
                  pl.BlockSpec((None, d, tn), lambda l, j: (l, 0, j)),
                  pl.BlockSpec((None, 1, tn), lambda l, j: (l, 0, j))],
        out_specs=pl.BlockSpec((None, rows, tn), lambda l, j: (l, 0, j)),
        out_shape=jax.ShapeDtypeStruct((depth, rows, n6), F32),
        compiler_params=_cparams(("parallel", "parallel")),
        name="ada",
    )(svec, ada_w, ada_b.reshape(depth, 1, n6))


def _inproj_kernel(x_ref, nw_ref, sh_ref, sc_ref, w_ref, o_ref):
    h = _rms(x_ref[...], nw_ref[...]) * (1.0 + sc_ref[...]) + sh_ref[...]
    o_ref[...] = _dot(h.astype(BF16), w_ref[...])


def _mod_spec(d, chunk, nct):
    return pl.BlockSpec((None, None, 1, d), lambda b, i: (b, jnp.where(i >= nct, 1, 0), 0, chunk))


def _inproj(xs, nw, mod, w, tm, nct):
    bsz, t, d = xs.shape
    n = w.shape[1]
    return pl.pallas_call(
        _inproj_kernel,
        grid=(bsz, t // tm),
        in_specs=[pl.BlockSpec((None, tm, d), lambda b, i: (b, i, 0)),
                  pl.BlockSpec((1, d), lambda b, i: (0, 0)),
                  _mod_spec(d, 0, nct), _mod_spec(d, 1, nct),
                  pl.BlockSpec((d, n), lambda b, i: (0, 0))],
        out_specs=pl.BlockSpec((None, tm, n), lambda b, i: (b, i, 0)),
        out_shape=jax.ShapeDtypeStruct((bsz, t, n), F32),
        compiler_params=_cparams(("parallel", "parallel")),
        name="inproj",
    )(xs, nw.reshape(1, d), mod, mod, w)


def _conv_kernel(x_ref, w_ref, b_ref, o_ref, pad_ref, *, lc, l):
    ct = x_ref.shape[1]
    zeros = jnp.zeros((SUBLANES, ct), F32)
    pad_ref[0:SUBLANES, :] = zeros
    pad_ref[SUBLANES:SUBLANES + lc, :] = x_ref[0:lc, :]
    pad_ref[SUBLANES + lc:2 * SUBLANES + lc, :] = zeros
    pad_ref[2 * SUBLANES + lc:2 * SUBLANES + lc + l, :] = x_ref[lc:lc + l, :]
    pad_ref[2 * SUBLANES + lc + l:3 * SUBLANES + lc + l, :] = zeros
    w = w_ref[...]
    bias = b_ref[...]
    for r0 in range(0, lc + l, CHUNK):
        base = (SUBLANES if r0 < lc else 2 * SUBLANES) + r0
        acc = bias + w[0:1, :] * pad_ref[base - 2:base - 2 + CHUNK, :]
        for k in range(1, CONV_K):
            acc = acc + w[k:k + 1, :] * pad_ref[base + k - 2:base + k - 2 + CHUNK, :]
        o_ref[r0:r0 + CHUNK, :] = acc * _sigmoid(acc)


def _conv(p, col0, conv_w, conv_b, lc, l):
    bsz, t, _ = p.shape
    c = conv_w.shape[1]
    ct = 256
    cb0 = col0 // ct
    return pl.pallas_call(
        functools.partial(_conv_kernel, lc=lc, l=l),
        grid=(bsz, c // ct),
        in_specs=[pl.BlockSpec((None, t, ct), lambda b, j: (b, 0, cb0 + j)),
                  pl.BlockSpec((CONV_K, ct), lambda b, j: (0, j)),
                  pl.BlockSpec((1, ct), lambda b, j: (0, j))],
        out_specs=pl.BlockSpec((None, t, ct), lambda b, j: (b, 0, j)),
        out_shape=jax.ShapeDtypeStruct((bsz, t, c), F32),
        scratch_shapes=[pltpu.VMEM((t + 3 * SUBLANES, ct), F32)],
        compiler_params=_cparams(("parallel", "parallel")),
        name="dwconv_silu",
    )(p, conv_w, conv_b.reshape(1, c))


def _mla_prep_kernel(cq_ref, ckv_ref, ra_ref, rb_ref, cos_ref, sin_ref, qnw_ref, kvnw_ref,
                     wq_ref, wqs_ref, wk_ref, wva_ref, wvb_ref, q_ref, k_ref, va_ref, vb_ref):
    cos = cos_ref[...]
    sin = sin_ref[...]
    cos8 = jnp.concatenate([cos] * MLA_HEADS, axis=1)
    sin8 = jnp.concatenate([sin] * MLA_HEADS, axis=1)
    cqn = _rms(cq_ref[...], qnw_ref[...]).astype(BF16)
    q = _dot(cqn, wq_ref[...]) * cos8 + _dot(cqn, wqs_ref[...]) * sin8
    q_ref[...] = q.astype(BF16)
    ckvn = _rms(ckv_ref[...], kvnw_ref[...]).astype(BF16)
    kr = ra_ref[...] * cos + rb_ref[...] * sin
    k = _dot(ckvn, wk_ref[...]) + jnp.concatenate([kr] * MLA_HEADS, axis=1)
    k_ref[...] = k.astype(BF16)
    va_ref[...] = _dot(ckvn, wva_ref[...]).astype(BF16)
    vb_ref[...] = _dot(ckvn, wvb_ref[...]).astype(BF16)


def _mla_prep(p, cos_t, sin_t, qnw, kvnw, wq, wqs, wk, wva, wvb, tm):
    bsz, t, _ = p.shape
    hq = MLA_HEADS * HEAD_PAD
    hv = MLA_HEADS * MLA_V
    full = lambda shape: pl.BlockSpec(shape, lambda b, i: (0, 0))
    return pl.pallas_call(
        _mla_prep_kernel,
        grid=(bsz, t // tm),
        in_specs=[pl.BlockSpec((None, tm, Q_LORA), lambda b, i: (b, i, EV_CQ // Q_LORA)),
                  pl.BlockSpec((None, tm, KV_LORA), lambda b, i: (b, i, EV_CKV // KV_LORA)),
                  pl.BlockSpec((None, tm, LANES), lambda b, i: (b, i, EV_ROPE_A // LANES)),
                  pl.BlockSpec((None, tm, LANES), lambda b, i: (b, i, EV_ROPE_B // LANES)),
                  pl.BlockSpec((tm, LANES), lambda b, i: (i, 0)),
                  pl.BlockSpec((tm, LANES), lambda b, i: (i, 0)),
                  full((1, Q_LORA)), full((1, KV_LORA)),
                  full((Q_LORA, hq)), full((Q_LORA, hq)), full((KV_LORA, hq)),
                  full((KV_LORA, hv)), full((KV_LORA, hv))],
        out_specs=[pl.BlockSpec((None, tm, hq), lambda b, i: (b, i, 0)),
                   pl.BlockSpec((None, tm, hq), lambda b, i: (b, i, 0)),
                   pl.BlockSpec((None, tm, hv), lambda b, i: (b, i, 0)),
                   pl.BlockSpec((None, tm, hv), lambda b, i: (b, i, 0))],
        out_shape=[jax.ShapeDtypeStruct((bsz, t, hq), BF16), jax.ShapeDtypeStruct((bsz, t, hq), BF16),
                   jax.ShapeDtypeStruct((bsz, t, hv), BF16), jax.ShapeDtypeStruct((bsz, t, hv), BF16)],
        compiler_params=_cparams(("parallel", "parallel")),
        name="mla_prep",
    )(p, p, p, p, cos_t, sin_t, qnw.reshape(1, -1), kvnw.reshape(1, -1), wq, wqs, wk, wva, wvb)


def _attn_kernel(q_ref, k_ref, va_ref, vb_ref, o_ref, *, lc, nct):
    i = pl.program_id(2)

    def attend(nk):
        acc = None
        for e, v_ref in enumerate((va_ref, vb_ref)):
            qh = q_ref[:, e * HEAD_PAD:(e + 1) * HEAD_PAD]
            kh = k_ref[0:nk, e * HEAD_PAD:(e + 1) * HEAD_PAD]
            s = _dot_nt(qh, kh) * MLA_SCALE
            m = jnp.max(s, axis=-1, keepdims=True)
            p = jnp.exp(s - m)
            l = jnp.sum(p, axis=-1, keepdims=True)
            o = _dot(p.astype(BF16), v_ref[0:nk, :]) / l
            acc = o if acc is None else acc + o
        o_ref[...] = acc.astype(BF16)

    @pl.when(i < nct)
    def _():
        attend(lc)

    @pl.when(i >= nct)
    def _():
        attend(k_ref.shape[0])


def _attention(q, k, va, vb, tq, lc):
    bsz, t, _ = q.shape
    npair = MLA_HEADS // 2
    return pl.pallas_call(
        functools.partial(_attn_kernel, lc=lc, nct=lc // tq),
        grid=(bsz, npair, t // tq),
        in_specs=[pl.BlockSpec((None, tq, 2 * HEAD_PAD), lambda b, p, i: (b, i, p)),
                  pl.BlockSpec((None, t, 2 * HEAD_PAD), lambda b, p, i: (b, 0, p)),
                  pl.BlockSpec((None, t, 2 * MLA_V), lambda b, p, i: (b, 0, p)),
                  pl.BlockSpec((None, t, 2 * MLA_V), lambda b, p, i: (b, 0, p))],
        out_specs=pl.BlockSpec((None, tq, 2 * MLA_V), lambda b, p, i: (b, i, p)),
        out_shape=jax.ShapeDtypeStruct((bsz, t, MLA_HEADS * MLA_V), BF16),
        compiler_params=_cparams(("parallel", "parallel", "parallel")),
        name="mla_attention",
    )(q, k, va, vb)


def _chunk_of_step(s, n, ncc):
    r = s - n
    rev = jnp.where(r < ncc, ncc - 1 - r, n - 1 - (r - ncc))
    return jnp.where(s < n, s, rev)


def _scan_masks(rev):
    ti = lax.broadcasted_iota(jnp.int32, (CHUNK, CHUNK), 0)
    si = lax.broadcasted_iota(jnp.int32, (CHUNK, CHUNK), 1)
    eye = ti == si
    if rev:
        cum_m = (ti >= si).astype(BF16)
        mask = si >= ti
        last = si[0:1, :] == 0
    else:
        cum_m = (ti <= si).astype(BF16)
        mask = si <= ti
        last = si[0:1, :] == CHUNK - 1
    return cum_m, mask, eye, last


def _to_col(row, eye):
    return jnp.sum(jnp.where(eye, row, 0.0), axis=1, keepdims=True)


def _pick_last(row, last):
    return jnp.sum(jnp.where(last, row, 0.0), axis=1, keepdims=True)


def _ssd_kernel(xs_ref, bm_ref, cm_ref, z_ref, dt_ref, dtb_ref, alog_ref, dskip_ref, nw_ref,
                o_ref, ybuf, ht, *, n, ncc):
    s = pl.program_id(1)
    c = _chunk_of_step(s, n, ncc)
    hpg = SSM_HEADS // SSM_GROUPS
    gw = hpg * SSM_HEADDIM

    @pl.when(jnp.logical_or(s == 0, s == n))
    def _():
        ht[...] = jnp.zeros_like(ht)

    def chunk(rev):
        d = 1 if rev else 0
        cum_m, mask, eye, last = _scan_masks(rev)
        rows = slice(d * SSM_HEADS, (d + 1) * SSM_HEADS)
        dt = _softplus(dt_ref[rows, :] + dtb_ref[rows, :])
        da = dt * (-jnp.exp(alog_ref[rows, :]))
        cum = _dot_f32_rows(da, cum_m)
        head_of_lane = lax.broadcasted_iota(jnp.int32, (CHUNK, gw), 1) // SSM_HEADDIM
        xs = xs_ref[...]
        ys = []
        for g in range(SSM_GROUPS):
            bmb = bm_ref[:, g * SSM_STATE:(g + 1) * SSM_STATE].astype(BF16)
            cmb = cm_ref[:, g * SSM_STATE:(g + 1) * SSM_STATE].astype(BF16)
            cb = _dot_nt(cmb, bmb)
            xg = xs[:, g * gw:(g + 1) * gw]
            xgb = xg.astype(BF16)
            yg = jnp.zeros((CHUNK, gw), F32)
            eg = jnp.zeros((CHUNK, gw), F32)
            teg = jnp.zeros((CHUNK, gw), F32)
            dg = jnp.zeros((1, gw), F32)
            for r in range(hpg):
                h = g * hpg + r
                row = cum[h:h + 1, :]
                dtrow = dt[h:h + 1, :]
                col = _to_col(row, eye)
                dtcol = _to_col(dtrow, eye)
                tot = _pick_last(row, last)
                decay = jnp.exp(jnp.where(mask, col - row, -jnp.inf))
                w = cb * decay * dtrow
                yh = _dot(w.astype(BF16), xgb)
                sel = head_of_lane == r
                yg = jnp.where(sel, yh, yg)
                eg = jnp.where(sel, jnp.exp(col), eg)
                teg = jnp.where(sel, jnp.exp(tot - col) * dtcol, teg)
                dg = jnp.where(sel[0:1, :], jnp.exp(tot), dg)
            htg = ht[g]
            yg = yg + _dot(cmb, htg.astype(BF16)) * eg
            ht[g] = htg * dg + _dot_tn(bmb, (xg * teg).astype(BF16))
            ys.append(yg)
        return jnp.concatenate(ys, axis=1), xs

    @pl.when(s < n)
    def _():
        y, xs = chunk(False)
        ybuf[c] = y + dskip_ref[...] * xs

    @pl.when(s >= n)
    def _():
        y, _ = chunk(True)
        y = y + ybuf[c]
        z = z_ref[...]
        gated = y * (z * _sigmoid(z))
        outs = []
        for g in range(SSM_GROUPS):
            gg = gated[:, g * gw:(g + 1) * gw]
            outs.append(gg * lax.rsqrt(jnp.mean(gg * gg, axis=-1, keepdims=True) + EPS))
        o_ref[...] = (jnp.concatenate(outs, axis=1) * nw_ref[...]).astype(BF16)


def _ssd(cv, p, dt_t, dt_bias, a_log, d_skip, norm_w, lc):
    bsz, t, _ = cv.shape
    n = t // CHUNK
    ncc = lc // CHUNK
    cidx = lambda s: _chunk_of_step(s, n, ncc)
    cidx_out = lambda s: _chunk_of_step(jnp.maximum(s, n), n, ncc)
    nst = SSM_GROUPS * SSM_STATE
    small = lambda shape: pl.BlockSpec(shape, lambda b, s: (0, 0))
    return pl.pallas_call(
        functools.partial(_ssd_kernel, n=n, ncc=ncc),
        grid=(bsz, 2 * n),
        in_specs=[pl.BlockSpec((None, CHUNK, SSM_DINNER), lambda b, s: (b, cidx(s), 0)),
                  pl.BlockSpec((None, CHUNK, nst), lambda b, s: (b, cidx(s), SSM_DINNER // nst)),
                  pl.BlockSpec((None, CHUNK, nst), lambda b, s: (b, cidx(s), SSM_DINNER // nst + 1)),
                  pl.BlockSpec((None, CHUNK, SSM_DINNER), lambda b, s: (b, cidx_out(s), EV_Z // SSM_DINNER)),
                  pl.BlockSpec((None, 2 * SSM_HEADS, CHUNK), lambda b, s: (b, 0, cidx(s))),
                  small((2 * SSM_HEADS, 1)), small((2 * SSM_HEADS, 1)),
                  small((1, SSM_DINNER)), small((1, SSM_DINNER))],
        out_specs=pl.BlockSpec((None, CHUNK, SSM_DINNER), lambda b, s: (b, cidx_out(s), 0)),
        out_shape=jax.ShapeDtypeStruct((bsz, t, SSM_DINNER), BF16),
        scratch_shapes=[pltpu.VMEM((n, CHUNK, SSM_DINNER), F32),
                        pltpu.VMEM((SSM_GROUPS, SSM_STATE, SSM_DINNER // SSM_GROUPS), F32)],
        compiler_params=_cparams(("parallel", "arbitrary")),
        name="ssd_scan",
    )(cv, cv, cv, p, dt_t, dt_bias.reshape(-1, 1), a_log.reshape(-1, 1),
      jnp.repeat(d_skip, SSM_HEADDIM).reshape(1, -1), norm_w.reshape(1, -1))


def _mlstm_kernel(q_ref, k_ref, v_ref, o_ref, g_ref, ib_ref, fb_ref, nw_ref,
                  out_ref, hbuf, ctn, mst, *, n, ncc):
    s = pl.program_id(1)
    c = _chunk_of_step(s, n, ncc)
    nh = MLSTM_HEADS
    pw = 2 * MLSTM_DQK

    @pl.when(jnp.logical_or(s == 0, s == n))
    def _():
        ctn[...] = jnp.zeros_like(ctn)
        mst[...] = jnp.zeros_like(mst)

    def chunk(rev):
        d = 1 if rev else 0
        cum_m, mask, eye, last = _scan_masks(rev)
        ipre = g_ref[d * nh:(d + 1) * nh, :] + ib_ref[d * nh:(d + 1) * nh, :]
        fpre = g_ref[(2 + d) * nh:(3 + d) * nh, :] + fb_ref[d * nh:(d + 1) * nh, :]
        bcum = _dot_f32_rows(-_softplus(-fpre), cum_m)
        half_of_lane = lax.broadcasted_iota(jnp.int32, (CHUNK, pw), 1) // MLSTM_DQK
        half_of_row = lax.broadcasted_iota(jnp.int32, (pw, 1), 0) // MLSTM_DQK
        ones_col = (lax.broadcasted_iota(jnp.int32, (CHUNK, MLSTM_DV), 1) == 0).astype(F32)
        qv = q_ref[...]
        kv = k_ref[...] * (MLSTM_DQK ** -0.5)
        vv = v_ref[...]
        outs = []
        for p in range(nh // 2):
            qp = qv[:, p * pw:(p + 1) * pw]
            kp = kv[:, p * pw:(p + 1) * pw]
            ct_pair = ctn[p]
            ct_b = ct_pair.astype(BF16)
            new_pair = jnp.zeros_like(ct_pair)
            for e in range(2):
                h = 2 * p + e
                lsel = half_of_lane == e
                qm = jnp.where(lsel, qp, 0.0).astype(BF16)
                km = jnp.where(lsel, kp, 0.0).astype(BF16)
                brow = bcum[h:h + 1, :]
                irow = ipre[h:h + 1, :]
                m_in = mst[h:h + 1, 0:1]
                bcol = _to_col(brow, eye)
                btot = _pick_last(brow, last)
                logd = jnp.where(mask, bcol - brow + irow, -jnp.inf)
                gcol = bcol + m_in
                mt = jnp.maximum(gcol, jnp.max(logd, axis=1, keepdims=True))
                dmat = jnp.exp(logd - mt)
                keepq = jnp.exp(gcol - mt)
                amat = _dot_nt(qm, km) * dmat
                vext = jnp.concatenate([vv[:, h * MLSTM_DV:(h + 1) * MLSTM_DV], ones_col], axis=1)
                nd = _dot(amat.astype(BF16), vext.astype(BF16)) + keepq * _dot(qm, ct_b)
                num = nd[:, 0:MLSTM_DV]
                den = nd[:, MLSTM_DV:MLSTM_DV + 1]
                outs.append(num / jnp.maximum(jnp.abs(den), jnp.exp(-mt)))
                wend = btot - brow + irow
                m_new = jnp.maximum(btot + m_in, jnp.max(wend, axis=1, keepdims=True))
                keep = jnp.exp(btot + m_in - m_new)
                wcol = _to_col(jnp.exp(wend - m_new), eye)
                upd = _dot_tn(km, (vext * wcol).astype(BF16))
                new_pair = new_pair + upd + jnp.where(half_of_row == e, keep * ct_pair, 0.0)
                mst[h:h + 1, :] = jnp.broadcast_to(m_new, (1, LANES))
            ctn[p] = new_pair
        return jnp.concatenate(outs, axis=1)

    @pl.when(s < n)
    def _():
        hbuf[c] = chunk(False)

    @pl.when(s >= n)
    def _():
        ht = chunk(True) + hbuf[c]
        outs = []
        for h in range(nh):
            hh = ht[:, h * MLSTM_DV:(h + 1) * MLSTM_DV]
            outs.append(hh * lax.rsqrt(jnp.mean(hh * hh, axis=-1, keepdims=True) + EPS))
        hn = jnp.concatenate(outs, axis=1) * nw_ref[...]
        out_ref[...] = (_sigmoid(o_ref[...]) * hn).astype(BF16)


def _mlstm(cv, p, g_t, i_bias, f_bias, head_norm_w, lc):
    bsz, t, _ = cv.shape
    n = t // CHUNK
    ncc = lc // CHUNK
    cidx = lambda s: _chunk_of_step(s, n, ncc)
    cidx_out = lambda s: _chunk_of_step(jnp.maximum(s, n), n, ncc)
    nh = MLSTM_HEADS
    small = lambda shape: pl.BlockSpec(shape, lambda b, s: (0, 0))
    return pl.pallas_call(
        functools.partial(_mlstm_kernel, n=n, ncc=ncc),
        grid=(bsz, 2 * n),
        in_specs=[pl.BlockSpec((None, CHUNK, MLSTM_QK), lambda b, s: (b, cidx(s), 0)),
                  pl.BlockSpec((None, CHUNK, MLSTM_QK), lambda b, s: (b, cidx(s), 1)),
                  pl.BlockSpec((None, CHUNK, MLSTM_VW), lambda b, s: (b, cidx(s), OD_V // MLSTM_VW)),
                  pl.BlockSpec((None, CHUNK, MLSTM_VW), lambda b, s: (b, cidx_out(s), OD_O // MLSTM_VW)),
                  pl.BlockSpec((None, 4 * nh, CHUNK), lambda b, s: (b, 0, cidx(s))),
                  small((2 * nh, 1)), small((2 * nh, 1)), small((1, MLSTM_VW))],
        out_specs=pl.BlockSpec((None, CHUNK, MLSTM_VW), lambda b, s: (b, cidx_out(s), 0)),
        out_shape=jax.ShapeDtypeStruct((bsz, t, MLSTM_VW), BF16),
        scratch_shapes=[pltpu.VMEM((n, CHUNK, MLSTM_VW), F32),
                        pltpu.VMEM((nh // 2, 2 * MLSTM_DQK, 2 * MLSTM_DV), F32),
                        pltpu.VMEM((nh, LANES), F32)],
        compiler_params=_cparams(("parallel", "arbitrary")),
        name="mlstm_scan",
    )(cv, cv, p, p, g_t, i_bias.reshape(-1, 1), f_bias.reshape(-1, 1), head_norm_w.reshape(1, -1))


def _outproj_kernel(*refs, nin):
    ms = refs[:nin]
    ws = refs[nin:2 * nin]
    x_ref, g_ref, o_ref = refs[2 * nin:]
    acc = _dot(ms[0][...], ws[0][...])
    for m_ref, w_ref in zip(ms[1:], ws[1:]):
        acc = acc + _dot(m_ref[...], w_ref[...])
    o_ref[...] = x_ref[...] + g_ref[...] * acc


def _outproj(ms, ws, xs, mod, tm, nct):
    bsz, t, d = xs.shape
    nin = len(ms)
    return pl.pallas_call(
        functools.partial(_outproj_kernel, nin=nin),
        grid=(bsz, t // tm),
        in_specs=[pl.BlockSpec((None, tm, m.shape[2]), lambda b, i: (b, i, 0)) for m in ms]
        + [pl.BlockSpec(w.shape, lambda b, i: (0, 0)) for w in ws]
        + [pl.BlockSpec((None, tm, d), lambda b, i: (b, i, 0)), _mod_spec(d, 2, nct)],
        out_specs=pl.BlockSpec((None, tm, d), lambda b, i: (b, i, 0)),
        out_shape=jax.ShapeDtypeStruct((bsz, t, d), F32),
        compiler_params=_cparams(("parallel", "parallel")),
        name="outproj_residual",
    )(*ms, *ws, xs, mod)


def _mlp_kernel(x_ref, nw_ref, sh_ref, sc_ref, g_ref, w1_ref, w2_ref, o_ref, *, tf):
    x = x_ref[...]
    h = (_rms(x, nw_ref[...]) * (1.0 + sc_ref[...]) + sh_ref[...]).astype(BF16)
    acc = jnp.zeros(x.shape, F32)
    for f in range(0, w1_ref.shape[1], tf):
        u = jnp.maximum(_dot(h, w1_ref[:, f:f + tf]), 0.0)
        acc = acc + _dot((u * u).astype(BF16), w2_ref[f:f + tf, :])
    o_ref[...] = x + g_ref[...] * acc


def _mlp(xs, nw, mod, w1, w2, tm, nct):
    bsz, t, d = xs.shape
    dff = w1.shape[1]
    return pl.pallas_call(
        functools.partial(_mlp_kernel, tf=1024),
        grid=(bsz, t // tm),
        in_specs=[pl.BlockSpec((None, tm, d), lambda b, i: (b, i, 0)),
                  pl.BlockSpec((1, d), lambda b, i: (0, 0)),
                  _mod_spec(d, 3, nct), _mod_spec(d, 4, nct), _mod_spec(d, 5, nct),
                  pl.BlockSpec((d, dff), lambda b, i: (0, 0)),
                  pl.BlockSpec((dff, d), lambda b, i: (0, 0))],
        out_specs=pl.BlockSpec((None, tm, d), lambda b, i: (b, i, 0)),
        out_shape=jax.ShapeDtypeStruct((bsz, t, d), F32),
        compiler_params=_cparams(("parallel", "parallel")),
        name="mlp_residual",
    )(xs, nw.reshape(1, d), mod, mod, mod, w1, w2)


def _final_kernel(x_ref, w_ref, o_ref):
    o_ref[...] = _rms(x_ref[...], w_ref[...])


def _final_norm(xs, w, tm, nct, l):
    bsz, _, d = xs.shape
    return pl.pallas_call(
        _final_kernel,
        grid=(bsz, l // tm),
        in_specs=[pl.BlockSpec((None, tm, d), lambda b, i: (b, i + nct, 0)),
                  pl.BlockSpec((1, d), lambda b, i: (0, 0))],
        out_specs=pl.BlockSpec((None, tm, d), lambda b, i: (b, i, 0)),
        out_shape=jax.ShapeDtypeStruct((bsz, l, d), F32),
        compiler_params=_cparams(("parallel", "parallel")),
        name="final_norm",
    )(xs, w.reshape(1, d))


def _rope_swap(w):
    j = np.arange(MLA_ROPE)
    first = (j % 16) < 8
    src = np.where(first, j + 8, j - 8)
    sign = np.where(first, -1.0, 1.0).astype(np.float32)
    return w[..., src] * sign


def _rope_tables(lc, l):
    n_rows = l // GRID_W
    row = jnp.repeat(jnp.arange(n_rows), GRID_W).astype(F32)
    col = jnp.tile(jnp.arange(GRID_W), n_rows).astype(F32)
    half = MLA_ROPE // 2
    inv = 1.0 / (ROPE_THETA ** (jnp.arange(0, half, 2, dtype=F32) / half))
    ar = row[:, None] * inv
    ac = col[:, None] * inv
    cos_l = jnp.concatenate([jnp.ones((l, MLA_NOPE), F32), jnp.cos(ar), jnp.cos(ar), jnp.cos(ac), jnp.cos(ac),
                             jnp.ones((l, HEAD_PAD - MLA_NOPE - MLA_ROPE), F32)], axis=1)
    sin_l = jnp.concatenate([jnp.zeros((l, MLA_NOPE), F32), jnp.sin(ar), jnp.sin(ar), jnp.sin(ac), jnp.sin(ac),
                             jnp.zeros((l, HEAD_PAD - MLA_NOPE - MLA_ROPE), F32)], axis=1)
    cos_t = jnp.concatenate([jnp.ones((lc, HEAD_PAD), F32), cos_l], axis=0)
    sin_t = jnp.concatenate([jnp.zeros((lc, HEAD_PAD), F32), sin_l], axis=0)
    return cos_t, sin_t


def _even_weights(w_in, w_uq, w_ukv, w_out):
    d = w_in.shape[0]
    o = np.cumsum([0, Q_LORA, KV_LORA, MLA_ROPE, SSM_DINNER, SSM_XBC, 2 * SSM_HEADS])
    cq, ckv, kr, z, xbc, dt = (w_in[:, o[i]:o[i + 1]] for i in range(6))
    zc = lambda n: jnp.zeros((d, n), F32)
    w_in_p = jnp.concatenate(
        [cq, ckv, z, xbc,
         zc(MLA_NOPE), kr, zc(HEAD_PAD - MLA_NOPE - MLA_ROPE),
         zc(MLA_NOPE), _rope_swap(kr), zc(HEAD_PAD - MLA_NOPE - MLA_ROPE),
         dt, zc(LANES - 2 * SSM_HEADS)], axis=1).astype(BF16)
    assert w_in_p.shape[1] == EV_N
    uq = w_uq.reshape(Q_LORA, MLA_HEADS, MLA_NOPE + MLA_ROPE)
    q_nope, q_rope = uq[..., :MLA_NOPE], uq[..., MLA_NOPE:]
    zq = lambda n: jnp.zeros((Q_LORA, MLA_HEADS, n), F32)
    npad = HEAD_PAD - MLA_NOPE - MLA_ROPE
    wq = jnp.concatenate([q_nope, q_rope, zq(npad)], axis=-1).reshape(Q_LORA, -1).astype(BF16)
    wqs = jnp.concatenate([zq(MLA_NOPE), _rope_swap(q_rope), zq(npad)], axis=-1).reshape(Q_LORA, -1).astype(BF16)
    ukv = w_ukv.reshape(KV_LORA, MLA_HEADS, MLA_NOPE + MLA_V)
    k_nope, vw = ukv[..., :MLA_NOPE], ukv[..., MLA_NOPE:]
    wk = jnp.concatenate([k_nope, jnp.zeros((KV_LORA, MLA_HEADS, HEAD_PAD - MLA_NOPE), F32)], axis=-1)
    wk = wk.reshape(KV_LORA, -1).astype(BF16)
    vpair = vw.reshape(KV_LORA, MLA_HEADS // 2, 2, MLA_V)
    zv = jnp.zeros((KV_LORA, MLA_HEADS // 2, MLA_V), F32)
    wva = jnp.concatenate([vpair[:, :, 0], zv], axis=-1).reshape(KV_LORA, -1).astype(BF16)
    wvb = jnp.concatenate([zv, vpair[:, :, 1]], axis=-1).reshape(KV_LORA, -1).astype(BF16)
    hv = MLA_HEADS * MLA_V
    return w_in_p, wq, wqs, wk, wva, wvb, w_out[:hv].astype(BF16), w_out[hv:].astype(BF16)


def _odd_weights(w_in):
    d = w_in.shape[0]
    ngate = 4 * MLSTM_HEADS
    w = jnp.concatenate([w_in, jnp.zeros((d, OD_N - OD_G - ngate), F32)], axis=1).astype(BF16)
    assert w.shape[1] == OD_N
    return w


def kernel(x, c, ctx, c_ctx, ada_w, ada_b, norm_mix_w, norm_mlp_w, mlp_w1, mlp_w2, ev_w_in, ev_q_norm_w, ev_w_uq, ev_kv_norm_w, ev_w_ukv, ev_conv_w, ev_conv_b, ev_dt_bias, ev_a_log, ev_d_skip, ev_ssm_norm_w, ev_w_out, od_w_in, od_conv_w, od_conv_b, od_i_bias, od_f_bias, od_head_norm_w, od_w_out, final_norm_w):
    bsz, l, d = x.shape
    lc = ctx.shape[1]
    depth = ada_w.shape[0]
    tm = 256 if lc % 256 == 0 else CHUNK
    assert lc % tm == 0 and l % tm == 0 and l % GRID_W == 0
    nct = lc // tm

    rows = -(-(bsz + 1) // SUBLANES) * SUBLANES
    svec = jnp.concatenate([c, c_ctx[None], jnp.zeros((rows - bsz - 1, d), F32)], axis=0)
    mods = _ada(svec, ada_w, ada_b)
    mod_ctx = jnp.broadcast_to(mods[:, bsz][:, None], (depth, bsz, 6 * d))
    mod_all = jnp.stack([mod_ctx, mods[:, :bsz]], axis=2)[:, :, :, None, :]

    cos_t, sin_t = _rope_tables(lc, l)
    xs = jnp.concatenate([ctx, x], axis=1)

    for layer in range(depth):
        mod = mod_all[layer]
        if layer % 2 == 0:
            e = layer // 2
            w_in_p, wq, wqs, wk, wva, wvb, wo_a, wo_s = _even_weights(ev_w_in[e], ev_w_uq[e], ev_w_ukv[e], ev_w_out[e])
            p = _inproj(xs, norm_mix_w[layer], mod, w_in_p, tm, nct)
            q, k, va, vb = _mla_prep(p, cos_t, sin_t, ev_q_norm_w[e], ev_kv_norm_w[e], wq, wqs, wk, wva, wvb, tm)
            o_attn = _attention(q, k, va, vb, tm, lc)
            cv = _conv(p, EV_XBC, ev_conv_w[e], ev_conv_b[e], lc, l)
            dt_t = jnp.swapaxes(p[:, :, EV_DT:EV_DT + 2 * SSM_HEADS], 1, 2)
            s_ssd = _ssd(cv, p, dt_t, ev_dt_bias[e], ev_a_log[e], ev_d_skip[e], ev_ssm_norm_w[e], lc)
            xs = _outproj([o_attn, s_ssd], [wo_a, wo_s], xs, mod, tm, nct)
        else:
            o = layer // 2
            p = _inproj(xs, norm_mix_w[layer], mod, _odd_weights(od_w_in[o]), tm, nct)
            cv = _conv(p, OD_QK, od_conv_w[o], od_conv_b[o], lc, l)
            g_t = jnp.swapaxes(p[:, :, OD_G:OD_G + 4 * MLSTM_HEADS], 1, 2)
            m = _mlstm(cv, p, g_t, od_i_bias[o], od_f_bias[o], od_head_norm_w[o], lc)
            xs = _outproj([m], [od_w_out[o].astype(BF16)], xs, mod, tm, nct)
        xs = _mlp(xs, norm_mlp_w[layer], mod, mlp_w1[layer].astype(BF16), mlp_w2[layer].astype(BF16), tm, nct)
    return _final_norm(xs, final_norm_w, tm, nct, l)
```

```python
import functools

import jax
import jax.numpy as jnp
import numpy as np
from jax import lax
from jax.experimental import pallas as pl
from jax.experimental.pallas import tpu as pltpu

F32 = jnp.float32
BF16 = jnp.bfloat16

EPS = 1e-6
GRID_W = 64
CHUNK = 128
CONV_K = 5
LANES = 128
SUBLANES = 8

MLA_HEADS = 8
MLA_NOPE = 64
MLA_ROPE = 32
MLA_V = 64
Q_LORA = 256
KV_LORA = 256
ROPE_THETA = 10000.0
MLA_SCALE = (MLA_NOPE + MLA_ROPE) ** -0.5
HEAD_PAD = 128

SSM_HEADS = 8
SSM_HEADDIM = 64
SSM_DINNER = SSM_HEADS * SSM_HEADDIM
SSM_GROUPS = 2
SSM_STATE = 128
SSM_XBC = SSM_DINNER + 2 * SSM_GROUPS * SSM_STATE

MLSTM_HEADS = 8
MLSTM_DQK = 64
MLSTM_DV = 128
MLSTM_QK = MLSTM_HEADS * MLSTM_DQK
MLSTM_VW = MLSTM_HEADS * MLSTM_DV

VMEM_LIMIT = 56 * 1024 * 1024

EV_CQ, EV_CKV, EV_Z, EV_XBC = 0, 256, 512, 1024
EV_ROPE_A, EV_ROPE_B, EV_DT, EV_N = 2048, 2176, 2304, 2432
OD_QK, OD_V, OD_O, OD_G, OD_N = 0, 1024, 2048, 3072, 3200


def _cparams(sem):
    return pltpu.CompilerParams(dimension_semantics=sem, vmem_limit_bytes=VMEM_LIMIT)


def _dot(a, b):
    return jnp.dot(a, b, preferred_element_type=F32)


def _dot_nt(a, b):
    return lax.dot_general(a, b, (((1,), (1,)), ((), ())), preferred_element_type=F32)


def _dot_tn(a, b):
    return lax.dot_general(a, b, (((0,), (0,)), ((), ())), preferred_element_type=F32)


def _sigmoid(x):
    return 1.0 / (1.0 + jnp.exp(-x))


def _softplus(x):
    return jnp.maximum(x, 0.0) + jnp.log(1.0 + jnp.exp(-jnp.abs(x)))


def _rms(x, w):
    ms = jnp.mean(x * x, axis=-1, keepdims=True)
    return x * lax.rsqrt(ms + EPS) * w


def _dot_f32_rows(x, m_bf16):
    hi = x.astype(BF16)
    r1 = x - hi.astype(F32)
    mid = r1.astype(BF16)
    lo = (r1 - mid.astype(F32)).astype(BF16)
    return _dot(hi, m_bf16) + _dot(mid, m_bf16) + _dot(lo, m_bf16)


def _ada_kernel(s_ref, w_ref, b_ref, o_ref):
    s = s_ref[...]
    s = (s * _sigmoid(s)).astype(BF16)
    o_ref[...] = _dot(s, w_ref[...].astype(BF16)) + b_ref[...]


def _ada(svec, ada_w, ada_b):
    depth, d, n6 = ada_w.shape
    rows = svec.shape[0]
    tn = 1024
    return pl.pallas_call(
        _ada_kernel,
        grid=(depth, n6 // tn),
        in_specs=[pl.BlockSpec((rows, d), lambda l, j: (0, 0)),
                  pl.BlockSpec((None, d, tn), lambda l, j: (l, 0, j)),
                  pl.BlockSpec((None, 1, tn), lambda l, j: (l, 0, j))],
        out_specs=pl.BlockSpec((None, rows, tn), lambda l, j: (l, 0, j)),
        out_shape=jax.ShapeDtypeStruct((depth, rows, n6), F32),
        compiler_params=_cparams(("parallel", "parallel")),
        name="ada",
    )(svec, ada_w, ada_b.reshape(depth, 1, n6))


def _inproj_kernel(x_ref, nw_ref, sh_ref, sc_ref, w_ref, o_ref):
    h = _rms(x_ref[...], nw_ref[...]) * (1.0 + sc_ref[...]) + sh_ref[...]
    o_ref[...] = _dot(h.astype(BF16), w_ref[...])


def _mod_spec(d, chunk, nct):
    return pl.BlockSpec((None, None, 1, d), lambda b, i: (b, jnp.where(i >= nct, 1, 0), 0, chunk))


def _inproj(xs, nw, mod, w, tm, nct):
    bsz, t, d = xs.shape
    n = w.shape[1]
    return pl.pallas_call(
        _inproj_kernel,
        grid=(bsz, t // tm),
        in_specs=[pl.BlockSpec((None, tm, d), lambda b, i: (b, i, 0)),
                  pl.BlockSpec((1, d), lambda b, i: (0, 0)),
                  _mod_spec(d, 0, nct), _mod_spec(d, 1, nct),
                  pl.BlockSpec((d, n), lambda b, i: (0, 0))],
        out_specs=pl.BlockSpec((None, tm, n), lambda b, i: (b, i, 0)),
        out_shape=jax.ShapeDtypeStruct((bsz, t, n), F32),
        compiler_params=_cparams(("parallel", "parallel")),
        name="inproj",
    )(xs, nw.reshape(1, d), mod, mod, w)


def _conv_kernel(x_ref, w_ref, b_ref, o_ref, pad_ref, *, lc, l):
    ct = x_ref.shape[1]
    zeros = jnp.zeros((SUBLANES, ct), F32)
    pad_ref[0:SUBLANES, :] = zeros
    pad_ref[SUBLANES:SUBLANES + lc, :] = x_ref[0:lc, :]
    pad_ref[SUBLANES + lc:2 * SUBLANES + lc, :] = zeros
    pad_ref[2 * SUBLANES + lc:2 * SUBLANES + lc + l, :] = x_ref[lc:lc + l, :]
    pad_ref[2 * SUBLANES + lc + l:3 * SUBLANES + lc + l, :] = zeros
    w = w_ref[...]
    bias = b_ref[...]
    for r0 in range(0, lc + l, CHUNK):
        base = (SUBLANES if r0 < lc else 2 * SUBLANES) + r0
        acc = bias + w[0:1, :] * pad_ref[base - 2:base - 2 + CHUNK, :]
        for k in range(1, CONV_K):
            acc = acc + w[k:k + 1, :] * pad_ref[base + k - 2:base + k - 2 + CHUNK, :]
        o_ref[r0:r0 + CHUNK, :] = acc * _sigmoid(acc)


def _conv(p, col0, conv_w, conv_b, lc, l):
    bsz, t, _ = p.shape
    c = conv_w.shape[1]
    ct = 256
    cb0 = col0 // ct
    return pl.pallas_call(
        functools.partial(_conv_kernel, lc=lc, l=l),
        grid=(bsz, c // ct),
        in_specs=[pl.BlockSpec((None, t, ct), lambda b, j: (b, 0, cb0 + j)),
                  pl.BlockSpec((CONV_K, ct), lambda b, j: (0, j)),
                  pl.BlockSpec((1, ct), lambda b, j: (0, j))],
        out_specs=pl.BlockSpec((None, t, ct), lambda b, j: (b, 0, j)),
        out_shape=jax.ShapeDtypeStruct((bsz, t, c), F32),
        scratch_shapes=[pltpu.VMEM((t + 3 * SUBLANES, ct), F32)],
        compiler_params=_cparams(("parallel", "parallel")),
        name="dwconv_silu",
    )(p, conv_w, conv_b.reshape(1, c))


def _mla_prep_kernel(cq_ref, ckv_ref, ra_ref, rb_ref, cos_ref, sin_ref, qnw_ref, kvnw_ref,
                     wq_ref, wqs_ref, wk_ref, wva_ref, wvb_ref, q_ref, k_ref, va_ref, vb_ref):
    cos = cos_ref[...]
    sin = sin_ref[...]
    cos8 = jnp.concatenate([cos] * MLA_HEADS, axis=1)
    sin8 = jnp.concatenate([sin] * MLA_HEADS, axis=1)
    cqn = _rms(cq_ref[...], qnw_ref[...]).astype(BF16)
    q = _dot(cqn, wq_ref[...]) * cos8 + _dot(cqn, wqs_ref[...]) * sin8
    q_ref[...] = q.astype(BF16)
    ckvn = _rms(ckv_ref[...], kvnw_ref[...]).astype(BF16)
    kr = ra_ref[...] * cos + rb_ref[...] * sin
    k = _dot(ckvn, wk_ref[...]) + jnp.concatenate([kr] * MLA_HEADS, axis=1)
    k_ref[...] = k.astype(BF16)
    va_ref[...] = _dot(ckvn, wva_ref[...]).astype(BF16)
    vb_ref[...] = _dot(ckvn, wvb_ref[...]).astype(BF16)


def _mla_prep(p, cos_t, sin_t, qnw, kvnw, wq, wqs, wk, wva, wvb, tm):
    bsz, t, _ = p.shape
    hq = MLA_HEADS * HEAD_PAD
    hv = MLA_HEADS * MLA_V
    full = lambda shape: pl.BlockSpec(shape, lambda b, i: (0, 0))
    return pl.pallas_call(
        _mla_prep_kernel,
        grid=(bsz, t // tm),
        in_specs=[pl.BlockSpec((None, tm, Q_LORA), lambda b, i: (b, i, EV_CQ // Q_LORA)),
                  pl.BlockSpec((None, tm, KV_LORA), lambda b, i: (b, i, EV_CKV // KV_LORA)),
                  pl.BlockSpec((None, tm, LANES), lambda b, i: (b, i, EV_ROPE_A // LANES)),
                  pl.BlockSpec((None, tm, LANES), lambda b, i: (b, i, EV_ROPE_B // LANES)),
                  pl.BlockSpec((tm, LANES), lambda b, i: (i, 0)),
                  pl.BlockSpec((tm, LANES), lambda b, i: (i, 0)),
                  full((1, Q_LORA)), full((1, KV_LORA)),
                  full((Q_LORA, hq)), full((Q_LORA, hq)), full((KV_LORA, hq)),
                  full((KV_LORA, hv)), full((KV_LORA, hv))],
        out_specs=[pl.BlockSpec((None, tm, hq), lambda b, i: (b, i, 0)),
                   pl.BlockSpec((None, tm, hq), lambda b, i: (b, i, 0)),
                   pl.BlockSpec((None, tm, hv), lambda b, i: (b, i, 0)),
                   pl.BlockSpec((None, tm, hv), lambda b, i: (b, i, 0))],
        out_shape=[jax.ShapeDtypeStruct((bsz, t, hq), BF16), jax.ShapeDtypeStruct((bsz, t, hq), BF16),
                   jax.ShapeDtypeStruct((bsz, t, hv), BF16), jax.ShapeDtypeStruct((bsz, t, hv), BF16)],
        compiler_params=_cparams(("parallel", "parallel")),
        name="mla_prep",
    )(p, p, p, p, cos_t, sin_t, qnw.reshape(1, -1), kvnw.reshape(1, -1), wq, wqs, wk, wva, wvb)


def _attn_kernel(q_ref, k_ref, va_ref, vb_ref, o_ref, *, lc, nct):
    i = pl.program_id(2)

    def attend(nk):
        acc = None
        for e, v_ref in enumerate((va_ref, vb_ref)):
            qh = q_ref[:, e * HEAD_PAD:(e + 1) * HEAD_PAD]
            kh = k_ref[0:nk, e * HEAD_PAD:(e + 1) * HEAD_PAD]
            s = _dot_nt(qh, kh) * MLA_SCALE
            m = jnp.max(s, axis=-1, keepdims=True)
            p = jnp.exp(s - m)
            l = jnp.sum(p, axis=-1, keepdims=True)
            o = _dot(p.astype(BF16), v_ref[0:nk, :]) / l
            acc = o if acc is None else acc + o
        o_ref[...] = acc.astype(BF16)

    @pl.when(i < nct)
    def _():
        attend(lc)

    @pl.when(i >= nct)
    def _():
        attend(k_ref.shape[0])


def _attention(q, k, va, vb, tq, lc):
    bsz, t, _ = q.shape
    npair = MLA_HEADS // 2
    return pl.pallas_call(
        functools.partial(_attn_kernel, lc=lc, nct=lc // tq),
        grid=(bsz, npair, t // tq),
        in_specs=[pl.BlockSpec((None, tq, 2 * HEAD_PAD), lambda b, p, i: (b, i, p)),
                  pl.BlockSpec((None, t, 2 * HEAD_PAD), lambda b, p, i: (b, 0, p)),
                  pl.BlockSpec((None, t, 2 * MLA_V), lambda b, p, i: (b, 0, p)),
                  pl.BlockSpec((None, t, 2 * MLA_V), lambda b, p, i: (b, 0, p))],
        out_specs=pl.BlockSpec((None, tq, 2 * MLA_V), lambda b, p, i: (b, i, p)),
        out_shape=jax.ShapeDtypeStruct((bsz, t, MLA_HEADS * MLA_V), BF16),
        compiler_params=_cparams(("parallel", "parallel", "parallel")),
        name="mla_attention",
    )(q, k, va, vb)


def _chunk_of_step(s, n, ncc):
    r = s - n
    rev = jnp.where(r < ncc, ncc - 1 - r, n - 1 - (r - ncc))
    return jnp.where(s < n, s, rev)


def _scan_masks(rev):
    ti = lax.broadcasted_iota(jnp.int32, (CHUNK, CHUNK), 0)
    si = lax.broadcasted_iota(jnp.int32, (CHUNK, CHUNK), 1)
    eye = ti == si
    if rev:
        cum_m = (ti >= si).astype(BF16)
        mask = si >= ti
        last = si[0:1, :] == 0
    else:
        cum_m = (ti <= si).astype(BF16)
        mask = si <= ti
        last = si[0:1, :] == CHUNK - 1
    return cum_m, mask, eye, last


def _to_col(row, eye):
    return jnp.sum(jnp.where(eye, row, 0.0), axis=1, keepdims=True)


def _pick_last(row, last):
    return jnp.sum(jnp.where(last, row, 0.0), axis=1, keepdims=True)


def _ssd_kernel(xs_ref, bm_ref, cm_ref, z_ref, dt_ref, dtb_ref, alog_ref, dskip_ref, nw_ref,
                o_ref, ybuf, ht, *, n, ncc):
    s = pl.program_id(1)
    c = _chunk_of_step(s, n, ncc)
    hpg = SSM_HEADS // SSM_GROUPS
    gw = hpg * SSM_HEADDIM

    @pl.when(jnp.logical_or(s == 0, s == n))
    def _():
        ht[...] = jnp.zeros_like(ht)

    def chunk(rev):
        d = 1 if rev else 0
        cum_m, mask, eye, last = _scan_masks(rev)
        rows = slice(d * SSM_HEADS, (d + 1) * SSM_HEADS)
        dt = _softplus(dt_ref[rows, :] + dtb_ref[rows, :])
        da = dt * (-jnp.exp(alog_ref[rows, :]))
        cum = _dot_f32_rows(da, cum_m)
        head_of_lane = lax.broadcasted_iota(jnp.int32, (CHUNK, gw), 1) // SSM_HEADDIM
        xs = xs_ref[...]
        ys = []
        for g in range(SSM_GROUPS):
            bmb = bm_ref[:, g * SSM_STATE:(g + 1) * SSM_STATE].astype(BF16)
            cmb = cm_ref[:, g * SSM_STATE:(g + 1) * SSM_STATE].astype(BF16)
            cb = _dot_nt(cmb, bmb)
            xg = xs[:, g * gw:(g + 1) * gw]
            xgb = xg.astype(BF16)
            yg = jnp.zeros((CHUNK, gw), F32)
            eg = jnp.zeros((CHUNK, gw), F32)
            teg = jnp.zeros((CHUNK, gw), F32)
            dg = jnp.zeros((1, gw), F32)
            for r in range(hpg):
                h = g * hpg + r
                row = cum[h:h + 1, :]
                dtrow = dt[h:h + 1, :]
                col = _to_col(row, eye)
                dtcol = _to_col(dtrow, eye)
                tot = _pick_last(row, last)
                decay = jnp.exp(jnp.where(mask, col - row, -jnp.inf))
                w = cb * decay * dtrow
                yh = _dot(w.astype(BF16), xgb)
                sel = head_of_lane == r
                yg = jnp.where(sel, yh, yg)
                eg = jnp.where(sel, jnp.exp(col), eg)
                teg = jnp.where(sel, jnp.exp(tot - col) * dtcol, teg)
                dg = jnp.where(sel[0:1, :], jnp.exp(tot), dg)
            htg = ht[g]
            yg = yg + _dot(cmb, htg.astype(BF16)) * eg
            ht[g] = htg * dg + _dot_tn(bmb, (xg * teg).astype(BF16))
            ys.append(yg)
        return jnp.concatenate(ys, axis=1), xs

    @pl.when(s < n)
    def _():
        y, xs = chunk(False)
        ybuf[c] = y + dskip_ref[...] * xs

    @pl.when(s >= n)
    def _():
        y, _ = chunk(True)
        y = y + ybuf[c]
        z = z_ref[...]
        gated = y * (z * _sigmoid(z))
        outs = []
        for g in range(SSM_GROUPS):
            gg = gated[:, g * gw:(g + 1) * gw]
            outs.append(gg * lax.rsqrt(jnp.mean(gg * gg, axis=-1, keepdims=True) + EPS))
        o_ref[...] = (jnp.concatenate(outs, axis=1) * nw_ref[...]).astype(BF16)


def _ssd(cv, p, dt_t, dt_bias, a_log, d_skip, norm_w, lc):
    bsz, t, _ = cv.shape
    n = t // CHUNK
    ncc = lc // CHUNK
    cidx = lambda s: _chunk_of_step(s, n, ncc)
    cidx_out = lambda s: _chunk_of_step(jnp.maximum(s, n), n, ncc)
    nst = SSM_GROUPS * SSM_STATE
    small = lambda shape: pl.BlockSpec(shape, lambda b, s: (0, 0))
    return pl.pallas_call(
        functools.partial(_ssd_kernel, n=n, ncc=ncc),
        grid=(bsz, 2 * n),
        in_specs=[pl.BlockSpec((None, CHUNK, SSM_DINNER), lambda b, s: (b, cidx(s), 0)),
                  pl.BlockSpec((None, CHUNK, nst), lambda b, s: (b, cidx(s), SSM_DINNER // nst)),
                  pl.BlockSpec((None, CHUNK, nst), lambda b, s: (b, cidx(s), SSM_DINNER // nst + 1)),
                  pl.BlockSpec((None, CHUNK, SSM_DINNER), lambda b, s: (b, cidx_out(s), EV_Z // SSM_DINNER)),
                  pl.BlockSpec((None, 2 * SSM_HEADS, CHUNK), lambda b, s: (b, 0, cidx(s))),
                  small((2 * SSM_HEADS, 1)), small((2 * SSM_HEADS, 1)),
                  small((1, SSM_DINNER)), small((1, SSM_DINNER))],
        out_specs=pl.BlockSpec((None, CHUNK, SSM_DINNER), lambda b, s: (b, cidx_out(s), 0)),
        out_shape=jax.ShapeDtypeStruct((bsz, t, SSM_DINNER), BF16),
        scratch_shapes=[pltpu.VMEM((n, CHUNK, SSM_DINNER), F32),
                        pltpu.VMEM((SSM_GROUPS, SSM_STATE, SSM_DINNER // SSM_GROUPS), F32)],
        compiler_params=_cparams(("parallel", "arbitrary")),
        name="ssd_scan",
    )(cv, cv, cv, p, dt_t, dt_bias.reshape(-1, 1), a_log.reshape(-1, 1),
      jnp.repeat(d_skip, SSM_HEADDIM).reshape(1, -1), norm_w.reshape(1, -1))


MLSTM_XR = 16


def _mlstm_kernel(k_ref, qt_ref, vt_ref, o_ref, g_ref, ib_ref, fb_ref, nw_ref,
                  out_ref, hbuf, cst, mst, *, n, ncc):
    s = pl.program_id(1)
    c = _chunk_of_step(s, n, ncc)
    nh, dk, dv = MLSTM_HEADS, MLSTM_DQK, MLSTM_DV
    pw = 2 * dk

    @pl.when(jnp.logical_or(s == 0, s == n))
    def _():
        cst[...] = jnp.zeros_like(cst)
        mst[...] = jnp.zeros_like(mst)

    def chunk(rev):
        d = 1 if rev else 0
        si = lax.broadcasted_iota(jnp.int32, (CHUNK, CHUNK), 0)
        ti = lax.broadcasted_iota(jnp.int32, (CHUNK, CHUNK), 1)
        seen = (si >= ti) if rev else (si <= ti)
        cum_m = seen.astype(BF16)
        ipre = g_ref[d * nh:(d + 1) * nh, :] + ib_ref[d * nh:(d + 1) * nh, :]
        fpre = g_ref[(2 + d) * nh:(3 + d) * nh, :] + fb_ref[d * nh:(d + 1) * nh, :]
        bcum = _dot_f32_rows(-_softplus(-fpre), cum_m)
        lane = lax.broadcasted_iota(jnp.int32, (1, CHUNK), 1)
        btot = jnp.sum(jnp.where(lane == (0 if rev else CHUNK - 1), bcum, 0.0), axis=1, keepdims=True)
        cs = ipre - bcum
        wend = btot + cs
        m_in = mst[:, 0:1]
        m_new = jnp.maximum(btot + m_in, jnp.max(wend, axis=1, keepdims=True))
        keep = jnp.exp(btot + m_in - m_new)
        wt = jnp.exp(wend - m_new)
        mst[...] = jnp.broadcast_to(m_new, mst.shape)

        row_half = lax.broadcasted_iota(jnp.int32, (pw, CHUNK), 0) // dk
        lane_half = lax.broadcasted_iota(jnp.int32, (CHUNK, pw), 1) // dk
        ones_rows = (lax.broadcasted_iota(jnp.int32, (MLSTM_XR, CHUNK), 0) == 0).astype(F32)
        zblk = jnp.zeros((CHUNK, CHUNK), BF16)
        outs = []
        for p in range(nh // 2):
            kp = k_ref[:, p * pw:(p + 1) * pw] * (dk ** -0.5)
            qtp = qt_ref[p * pw:(p + 1) * pw, :]
            qbd = jnp.concatenate([jnp.where(row_half == 0, qtp, 0.0), jnp.where(row_half == 1, qtp, 0.0)],
                                  axis=1).astype(BF16)
            st = _dot(kp.astype(BF16), qbd)
            cp = cst[p]
            inter = _dot(cp.astype(BF16), qbd)
            ats, keepqs, mrows = [], [], []
            for e in range(2):
                h = 2 * p + e
                cs_st = jnp.transpose(jnp.broadcast_to(cs[h:h + 1, :], (CHUNK, CHUNK)))
                ex = jnp.where(seen, cs_st, -jnp.inf)
                mrow = jnp.maximum(m_in[h:h + 1, :], jnp.max(ex, axis=0, keepdims=True))
                ats.append((st[:, e * CHUNK:(e + 1) * CHUNK] * jnp.exp(ex - mrow)).astype(BF16))
                keepqs.append(jnp.exp(m_in[h:h + 1, :] - mrow))
                mrows.append(mrow)
            abd = jnp.concatenate([jnp.concatenate([ats[0], zblk], axis=1),
                                   jnp.concatenate([zblk, ats[1]], axis=1)], axis=0)
            vtp = jnp.concatenate(
                [jnp.concatenate([vt_ref[(2 * p + e) * dv:(2 * p + e + 1) * dv, :], ones_rows], axis=0)
                 for e in range(2)], axis=1)
            nd = _dot(vtp.astype(BF16), abd) + inter * jnp.concatenate(keepqs, axis=1)
            for e in range(2):
                h = 2 * p + e
                num = nd[0:dv, e * CHUNK:(e + 1) * CHUNK]
                den = nd[dv:dv + 1, e * CHUNK:(e + 1) * CHUNK]
                mt = bcum[h:h + 1, :] + mrows[e]
                outs.append(num / jnp.maximum(jnp.abs(den), jnp.exp(-mt)))
            wtp = jnp.concatenate([wt[2 * p:2 * p + 1, :], wt[2 * p + 1:2 * p + 2, :]], axis=1)
            kbd = jnp.concatenate([jnp.where(lane_half == 0, kp, 0.0), jnp.where(lane_half == 1, kp, 0.0)],
                                  axis=0).astype(BF16)
            keepp = jnp.where(lane_half[0:1, :] == 0, keep[2 * p:2 * p + 1, :], keep[2 * p + 1:2 * p + 2, :])
            cst[p] = cp * keepp + _dot((vtp * wtp).astype(BF16), kbd)
        return jnp.concatenate(outs, axis=0)

    @pl.when(s < n)
    def _():
        hbuf[c] = chunk(False)

    @pl.when(s >= n)
    def _():
        ht = chunk(True) + hbuf[c]
        outs = []
        for h in range(nh):
            hh = ht[h * dv:(h + 1) * dv, :]
            hn = hh * lax.rsqrt(jnp.mean(hh * hh, axis=0, keepdims=True) + EPS)
            outs.append(jnp.transpose(hn))
        hn = jnp.concatenate(outs, axis=1) * nw_ref[...]
        out_ref[...] = (_sigmoid(o_ref[...]) * hn).astype(BF16)


def _mlstm(cv, q_t, v_t, p, g_t, i_bias, f_bias, head_norm_w, lc):
    bsz, t, _ = cv.shape
    n = t // CHUNK
    ncc = lc // CHUNK
    cidx = lambda s: _chunk_of_step(s, n, ncc)
    cidx_out = lambda s: _chunk_of_step(jnp.maximum(s, n), n, ncc)
    nh = MLSTM_HEADS
    small = lambda shape: pl.BlockSpec(shape, lambda b, s: (0, 0))
    return pl.pallas_call(
        functools.partial(_mlstm_kernel, n=n, ncc=ncc),
        grid=(bsz, 2 * n),
        in_specs=[pl.BlockSpec((None, CHUNK, MLSTM_QK), lambda b, s: (b, cidx(s), 1)),
                  pl.BlockSpec((None, MLSTM_QK, CHUNK), lambda b, s: (b, 0, cidx(s))),
                  pl.BlockSpec((None, MLSTM_VW, CHUNK), lambda b, s: (b, 0, cidx(s))),
                  pl.BlockSpec((None, CHUNK, MLSTM_VW), lambda b, s: (b, cidx_out(s), OD_O // MLSTM_VW)),
                  pl.BlockSpec((None, 4 * nh, CHUNK), lambda b, s: (b, 0, cidx(s))),
                  small((2 * nh, 1)), small((2 * nh, 1)), small((1, MLSTM_VW))],
        out_specs=pl.BlockSpec((None, CHUNK, MLSTM_VW), lambda b, s: (b, cidx_out(s), 0)),
        out_shape=jax.ShapeDtypeStruct((bsz, t, MLSTM_VW), BF16),
        scratch_shapes=[pltpu.VMEM((n, MLSTM_VW, CHUNK), F32),
                        pltpu.VMEM((nh // 2, MLSTM_DV + MLSTM_XR, 2 * MLSTM_DQK), F32),
                        pltpu.VMEM((nh, LANES), F32)],
        compiler_params=_cparams(("parallel", "arbitrary")),
        name="mlstm_scan",
    )(cv, q_t, v_t, p, g_t, i_bias.reshape(-1, 1), f_bias.reshape(-1, 1), head_norm_w.reshape(1, -1))


def _outproj_kernel(*refs, nin):
    ms = refs[:nin]
    ws = refs[nin:2 * nin]
    x_ref, g_ref, o_ref = refs[2 * nin:]
    acc = _dot(ms[0][...], ws[0][...])
    for m_ref, w_ref in zip(ms[1:], ws[1:]):
        acc = acc + _dot(m_ref[...], w_ref[...])
    o_ref[...] = x_ref[...] + g_ref[...] * acc


def _outproj(ms, ws, xs, mod, tm, nct):
    bsz, t, d = xs.shape
    nin = len(ms)
    return pl.pallas_call(
        functools.partial(_outproj_kernel, nin=nin),
        grid=(bsz, t // tm),
        in_specs=[pl.BlockSpec((None, tm, m.shape[2]), lambda b, i: (b, i, 0)) for m in ms]
        + [pl.BlockSpec(w.shape, lambda b, i: (0, 0)) for w in ws]
        + [pl.BlockSpec((None, tm, d), lambda b, i: (b, i, 0)), _mod_spec(d, 2, nct)],
        out_specs=pl.BlockSpec((None, tm, d), lambda b, i: (b, i, 0)),
        out_shape=jax.ShapeDtypeStruct((bsz, t, d), F32),
        compiler_params=_cparams(("parallel", "parallel")),
        name="outproj_residual",
    )(*ms, *ws, xs, mod)


def _mlp_kernel(x_ref, nw_ref, sh_ref, sc_ref, g_ref, w1_ref, w2_ref, o_ref, *, tf):
    x = x_ref[...]
    h = (_rms(x, nw_ref[...]) * (1.0 + sc_ref[...]) + sh_ref[...]).astype(BF16)
    acc = jnp.zeros(x.shape, F32)
    for f in range(0, w1_ref.shape[1], tf):
        u = jnp.maximum(_dot(h, w1_ref[:, f:f + tf]), 0.0)
        acc = acc + _dot((u * u).astype(BF16), w2_ref[f:f + tf, :])
    o_ref[...] = x + g_ref[...] * acc


def _mlp(xs, nw, mod, w1, w2, tm, nct):
    bsz, t, d = xs.shape
    dff = w1.shape[1]
    return pl.pallas_call(
        functools.partial(_mlp_kernel, tf=1024),
        grid=(bsz, t // tm),
        in_specs=[pl.BlockSpec((None, tm, d), lambda b, i: (b, i, 0)),
                  pl.BlockSpec((1, d), lambda b, i: (0, 0)),
                  _mod_spec(d, 3, nct), _mod_spec(d, 4, nct), _mod_spec(d, 5, nct),
                  pl.BlockSpec((d, dff), lambda b, i: (0, 0)),
                  pl.BlockSpec((dff, d), lambda b, i: (0, 0))],
        out_specs=pl.BlockSpec((None, tm, d), lambda b, i: (b, i, 0)),
        out_shape=jax.ShapeDtypeStruct((bsz, t, d), F32),
        compiler_params=_cparams(("parallel", "parallel")),
        name="mlp_residual",
    )(xs, nw.reshape(1, d), mod, mod, mod, w1, w2)


def _final_kernel(x_ref, w_ref, o_ref):
    o_ref[...] = _rms(x_ref[...], w_ref[...])


def _final_norm(xs, w, tm, nct, l):
    bsz, _, d = xs.shape
    return pl.pallas_call(
        _final_kernel,
        grid=(bsz, l // tm),
        in_specs=[pl.BlockSpec((None, tm, d), lambda b, i: (b, i + nct, 0)),
                  pl.BlockSpec((1, d), lambda b, i: (0, 0))],
        out_specs=pl.BlockSpec((None, tm, d), lambda b, i: (b, i, 0)),
        out_shape=jax.ShapeDtypeStruct((bsz, l, d), F32),
        compiler_params=_cparams(("parallel", "parallel")),
        name="final_norm",
    )(xs, w.reshape(1, d))


def _rope_swap(w):
    j = np.arange(MLA_ROPE)
    first = (j % 16) < 8
    src = np.where(first, j + 8, j - 8)
    sign = np.where(first, -1.0, 1.0).astype(np.float32)
    return w[..., src] * sign


def _rope_tables(lc, l):
    n_rows = l // GRID_W
    row = jnp.repeat(jnp.arange(n_rows), GRID_W).astype(F32)
    col = jnp.tile(jnp.arange(GRID_W), n_rows).astype(F32)
    half = MLA_ROPE // 2
    inv = 1.0 / (ROPE_THETA ** (jnp.arange(0, half, 2, dtype=F32) / half))
    ar = row[:, None] * inv
    ac = col[:, None] * inv
    cos_l = jnp.concatenate([jnp.ones((l, MLA_NOPE), F32), jnp.cos(ar), jnp.cos(ar), jnp.cos(ac), jnp.cos(ac),
                             jnp.ones((l, HEAD_PAD - MLA_NOPE - MLA_ROPE), F32)], axis=1)
    sin_l = jnp.concatenate([jnp.zeros((l, MLA_NOPE), F32), jnp.sin(ar), jnp.sin(ar), jnp.sin(ac), jnp.sin(ac),
                             jnp.zeros((l, HEAD_PAD - MLA_NOPE - MLA_ROPE), F32)], axis=1)
    cos_t = jnp.concatenate([jnp.ones((lc, HEAD_PAD), F32), cos_l], axis=0)
    sin_t = jnp.concatenate([jnp.zeros((lc, HEAD_PAD), F32), sin_l], axis=0)
    return cos_t, sin_t


def _even_weights(w_in, w_uq, w_ukv, w_out):
    d = w_in.shape[0]
    o = np.cumsum([0, Q_LORA, KV_LORA, MLA_ROPE, SSM_DINNER, SSM_XBC, 2 * SSM_HEADS])
    cq, ckv, kr, z, xbc, dt = (w_in[:, o[i]:o[i + 1]] for i in range(6))
    zc = lambda n: jnp.zeros((d, n), F32)
    w_in_p = jnp.concatenate(
        [cq, ckv, z, xbc,
         zc(MLA_NOPE), kr, zc(HEAD_PAD - MLA_NOPE - MLA_ROPE),
         zc(MLA_NOPE), _rope_swap(kr), zc(HEAD_PAD - MLA_NOPE - MLA_ROPE),
         dt, zc(LANES - 2 * SSM_HEADS)], axis=1).astype(BF16)
    assert w_in_p.shape[1] == EV_N
    uq = w_uq.reshape(Q_LORA, MLA_HEADS, MLA_NOPE + MLA_ROPE)
    q_nope, q_rope = uq[..., :MLA_NOPE], uq[..., MLA_NOPE:]
    zq = lambda n: jnp.zeros((Q_LORA, MLA_HEADS, n), F32)
    npad = HEAD_PAD - MLA_NOPE - MLA_ROPE
    wq = jnp.concatenate([q_nope, q_rope, zq(npad)], axis=-1).reshape(Q_LORA, -1).astype(BF16)
    wqs = jnp.concatenate([zq(MLA_NOPE), _rope_swap(q_rope), zq(npad)], axis=-1).reshape(Q_LORA, -1).astype(BF16)
    ukv = w_ukv.reshape(KV_LORA, MLA_HEADS, MLA_NOPE + MLA_V)
    k_nope, vw = ukv[..., :MLA_NOPE], ukv[..., MLA_NOPE:]
    wk = jnp.concatenate([k_nope, jnp.zeros((KV_LORA, MLA_HEADS, HEAD_PAD - MLA_NOPE), F32)], axis=-1)
    wk = wk.reshape(KV_LORA, -1).astype(BF16)
    vpair = vw.reshape(KV_LORA, MLA_HEADS // 2, 2, MLA_V)
    zv = jnp.zeros((KV_LORA, MLA_HEADS // 2, MLA_V), F32)
    wva = jnp.concatenate([vpair[:, :, 0], zv], axis=-1).reshape(KV_LORA, -1).astype(BF16)
    wvb = jnp.concatenate([zv, vpair[:, :, 1]], axis=-1).reshape(KV_LORA, -1).astype(BF16)
    hv = MLA_HEADS * MLA_V
    return w_in_p, wq, wqs, wk, wva, wvb, w_out[:hv].astype(BF16), w_out[hv:].astype(BF16)


def _odd_weights(w_in):
    d = w_in.shape[0]
    ngate = 4 * MLSTM_HEADS
    w = jnp.concatenate([w_in, jnp.zeros((d, OD_N - OD_G - ngate), F32)], axis=1).astype(BF16)
    assert w.shape[1] == OD_N
    return w


def kernel(x, c, ctx, c_ctx, ada_w, ada_b, norm_mix_w, norm_mlp_w, mlp_w1, mlp_w2, ev_w_in, ev_q_norm_w, ev_w_uq, ev_kv_norm_w, ev_w_ukv, ev_conv_w, ev_conv_b, ev_dt_bias, ev_a_log, ev_d_skip, ev_ssm_norm_w, ev_w_out, od_w_in, od_conv_w, od_conv_b, od_i_bias, od_f_bias, od_head_norm_w, od_w_out, final_norm_w):
    bsz, l, d = x.shape
    lc = ctx.shape[1]
    depth = ada_w.shape[0]
    tm = 256 if lc % 256 == 0 else CHUNK
    assert lc % tm == 0 and l % tm == 0 and l % GRID_W == 0
    nct = lc // tm

    rows = -(-(bsz + 1) // SUBLANES) * SUBLANES
    svec = jnp.concatenate([c, c_ctx[None], jnp.zeros((rows - bsz - 1, d), F32)], axis=0)
    mods = _ada(svec, ada_w, ada_b)
    mod_ctx = jnp.broadcast_to(mods[:, bsz][:, None], (depth, bsz, 6 * d))
    mod_all = jnp.stack([mod_ctx, mods[:, :bsz]], axis=2)[:, :, :, None, :]

    cos_t, sin_t = _rope_tables(lc, l)
    xs = jnp.concatenate([ctx, x], axis=1)

    for layer in range(depth):
        mod = mod_all[layer]
        if layer % 2 == 0:
            e = layer // 2
            w_in_p, wq, wqs, wk, wva, wvb, wo_a, wo_s = _even_weights(ev_w_in[e], ev_w_uq[e], ev_w_ukv[e], ev_w_out[e])
            p = _inproj(xs, norm_mix_w[layer], mod, w_in_p, tm, nct)
            q, k, va, vb = _mla_prep(p, cos_t, sin_t, ev_q_norm_w[e], ev_kv_norm_w[e], wq, wqs, wk, wva, wvb, tm)
            o_attn = _attention(q, k, va, vb, tm, lc)
            cv = _conv(p, EV_XBC, ev_conv_w[e], ev_conv_b[e], lc, l)
            dt_t = jnp.swapaxes(p[:, :, EV_DT:EV_DT + 2 * SSM_HEADS], 1, 2)
            s_ssd = _ssd(cv, p, dt_t, ev_dt_bias[e], ev_a_log[e], ev_d_skip[e], ev_ssm_norm_w[e], lc)
            xs = _outproj([o_attn, s_ssd], [wo_a, wo_s], xs, mod, tm, nct)
        else:
            o = layer // 2
            p = _inproj(xs, norm_mix_w[layer], mod, _odd_weights(od_w_in[o]), tm, nct)
            cv = _conv(p, OD_QK, od_conv_w[o], od_conv_b[o], lc, l)
            g_t = jnp.swapaxes(p[:, :, OD_G:OD_G + 4 * MLSTM_HEADS], 1, 2)
            q_t = jnp.swapaxes(cv[:, :, :MLSTM_QK], 1, 2)
            v_t = jnp.swapaxes(p[:, :, OD_V:OD_V + MLSTM_VW], 1, 2)
            m = _mlstm(cv, q_t, v_t, p, g_t, od_i_bias[o], od_f_bias[o], od_head_norm_w[o], lc)
            xs = _outproj([m], [od_w_out[o].astype(BF16)], xs, mod, tm, nct)
        xs = _mlp(xs, norm_mlp_w[layer], mod, mlp_w1[layer].astype(BF16), mlp_w2[layer].astype(BF16), tm, nct)
    return _final_norm(xs, final_norm_w, tm, nct, l)
```

```python
import functools

import jax
import jax.numpy as jnp
import numpy as np
from jax import lax
from jax.experimental import pallas as pl
from jax.experimental.pallas import tpu as pltpu

F32 = jnp.float32
BF16 = jnp.bfloat16

EPS = 1e-6
GRID_W = 64
CHUNK = 128
CONV_K = 5
LANES = 128
SUBLANES = 8

MLA_HEADS = 8
MLA_NOPE = 64
MLA_ROPE = 32
MLA_V = 64
Q_LORA = 256
KV_LORA = 256
ROPE_THETA = 10000.0
MLA_SCALE = (MLA_NOPE + MLA_ROPE) ** -0.5
HEAD_PAD = 128
LOG2E = 1.4426950408889634
ATT_SUM_LANE_A = MLA_V
ATT_SUM_LANE_B = 0

SSM_HEADS = 8
SSM_HEADDIM = 64
SSM_DINNER = SSM_HEADS * SSM_HEADDIM
SSM_GROUPS = 2
SSM_STATE = 128
SSM_XBC = SSM_DINNER + 2 * SSM_GROUPS * SSM_STATE

MLSTM_HEADS = 8
MLSTM_DQK = 64
MLSTM_DV = 128
MLSTM_QK = MLSTM_HEADS * MLSTM_DQK
MLSTM_VW = MLSTM_HEADS * MLSTM_DV

VMEM_LIMIT = 56 * 1024 * 1024

EV_CQ, EV_CKV, EV_Z, EV_XBC = 0, 256, 512, 1024
EV_ROPE_A, EV_ROPE_B, EV_DT, EV_N = 2048, 2176, 2304, 2432
OD_QK, OD_V, OD_O, OD_G, OD_N = 0, 1024, 2048, 3072, 3200


def _cparams(sem):
    return pltpu.CompilerParams(dimension_semantics=sem, vmem_limit_bytes=VMEM_LIMIT)


def _dot(a, b):
    return jnp.dot(a, b, preferred_element_type=F32)


def _dot_nt(a, b):
    return lax.dot_general(a, b, (((1,), (1,)), ((), ())), preferred_element_type=F32)


def _dot_tn(a, b):
    return lax.dot_general(a, b, (((0,), (0,)), ((), ())), preferred_element_type=F32)


def _sigmoid(x):
    return 1.0 / (1.0 + jnp.exp(-x))


def _softplus(x):
    return jnp.maximum(x, 0.0) + jnp.log(1.0 + jnp.exp(-jnp.abs(x)))


def _rms(x, w):
    ms = jnp.mean(x * x, axis=-1, keepdims=True)
    return x * lax.rsqrt(ms + EPS) * w


def _dot_f32_rows(x, m_bf16):
    hi = x.astype(BF16)
    r1 = x - hi.astype(F32)
    mid = r1.astype(BF16)
    lo = (r1 - mid.astype(F32)).astype(BF16)
    return _dot(hi, m_bf16) + _dot(mid, m_bf16) + _dot(lo, m_bf16)


def _ada_kernel(s_ref, w_ref, b_ref, o_ref):
    s = s_ref[...]
    s = (s * _sigmoid(s)).astype(BF16)
    o_ref[...] = _dot(s, w_ref[...].astype(BF16)) + b_ref[...]


def _ada(svec, ada_w, ada_b):
    depth, d, n6 = ada_w.shape
    rows = svec.shape[0]
    tn = 1024
    return pl.pallas_call(
        _ada_kernel,
        grid=(depth, n6 // tn),
        in_specs=[pl.BlockSpec((rows, d), lambda l, j: (0, 0)),
                  pl.BlockSpec((None, d, tn), lambda l, j: (l, 0, j)),
                  pl.BlockSpec((None, 1, tn), lambda l, j: (l, 0, j))],
        out_specs=pl.BlockSpec((None, rows, tn), lambda l, j: (l, 0, j)),
        out_shape=jax.ShapeDtypeStruct((depth, rows, n6), F32),
        compiler_params=_cparams(("parallel", "parallel")),
        name="ada",
    )(svec, ada_w, ada_b.reshape(depth, 1, n6))


def _inproj_kernel(x_ref, nw_ref, sh_ref, sc_ref, w_ref, o_ref, *t_refs, tcols):
    h = _rms(x_ref[...], nw_ref[...]) * (1.0 + sc_ref[...]) + sh_ref[...]
    res = _dot(h.astype(BF16), w_ref[...])
    o_ref[...] = res
    if tcols is not None:
        t_refs[0][...] = jnp.transpose(res[:, tcols[0]:tcols[0] + tcols[1]])


def _mod_spec(d, chunk, nct):
    return pl.BlockSpec((None, None, 1, d), lambda b, i: (b, jnp.where(i >= nct, 1, 0), 0, chunk))


def _inproj(xs, nw, mod, w, tm, nct, tcols=None):
    bsz, t, d = xs.shape
    n = w.shape[1]
    out_specs = [pl.BlockSpec((None, tm, n), lambda b, i: (b, i, 0))]
    out_shape = [jax.ShapeDtypeStruct((bsz, t, n), F32)]
    if tcols is not None:
        out_specs.append(pl.BlockSpec((None, tcols[1], tm), lambda b, i: (b, 0, i)))
        out_shape.append(jax.ShapeDtypeStruct((bsz, tcols[1], t), F32))
    return pl.pallas_call(
        functools.partial(_inproj_kernel, tcols=tcols),
        grid=(bsz, t // tm),
        in_specs=[pl.BlockSpec((None, tm, d), lambda b, i: (b, i, 0)),
                  pl.BlockSpec((1, d), lambda b, i: (0, 0)),
                  _mod_spec(d, 0, nct), _mod_spec(d, 1, nct),
                  pl.BlockSpec((d, n), lambda b, i: (0, 0))],
        out_specs=out_specs,
        out_shape=out_shape,
        compiler_params=_cparams(("parallel", "parallel")),
        name="inproj",
    )(xs, nw.reshape(1, d), mod, mod, w)


def _conv_kernel(x_ref, w_ref, b_ref, o_ref, *rest, lc, l, ntt):
    pad_ref = rest[-1]
    ct = x_ref.shape[1]
    zeros = jnp.zeros((SUBLANES, ct), F32)
    pad_ref[0:SUBLANES, :] = zeros
    pad_ref[SUBLANES:SUBLANES + lc, :] = x_ref[0:lc, :]
    pad_ref[SUBLANES + lc:2 * SUBLANES + lc, :] = zeros
    pad_ref[2 * SUBLANES + lc:2 * SUBLANES + lc + l, :] = x_ref[lc:lc + l, :]
    pad_ref[2 * SUBLANES + lc + l:3 * SUBLANES + lc + l, :] = zeros
    w = w_ref[...]
    bias = b_ref[...]
    for r0 in range(0, lc + l, CHUNK):
        base = (SUBLANES if r0 < lc else 2 * SUBLANES) + r0
        acc = bias + w[0:1, :] * pad_ref[base - 2:base - 2 + CHUNK, :]
        for k in range(1, CONV_K):
            acc = acc + w[k:k + 1, :] * pad_ref[base + k - 2:base + k - 2 + CHUNK, :]
        o_ref[r0:r0 + CHUNK, :] = acc * _sigmoid(acc)
    if ntt:
        t_ref = rest[0]

        @pl.when(pl.program_id(1) < ntt)
        def _():
            for r0 in range(0, lc + l, CHUNK):
                t_ref[:, r0:r0 + CHUNK] = jnp.transpose(o_ref[r0:r0 + CHUNK, :])


def _conv(p, col0, conv_w, conv_b, lc, l, ntt=0):
    bsz, t, _ = p.shape
    c = conv_w.shape[1]
    ct = 256
    cb0 = col0 // ct
    out_specs = [pl.BlockSpec((None, t, ct), lambda b, j: (b, 0, j))]
    out_shape = [jax.ShapeDtypeStruct((bsz, t, c), F32)]
    if ntt:
        out_specs.append(pl.BlockSpec((None, ct, t), lambda b, j: (b, jnp.minimum(j, ntt - 1), 0)))
        out_shape.append(jax.ShapeDtypeStruct((bsz, ntt * ct, t), F32))
    return pl.pallas_call(
        functools.partial(_conv_kernel, lc=lc, l=l, ntt=ntt),
        grid=(bsz, c // ct),
        in_specs=[pl.BlockSpec((None, t, ct), lambda b, j: (b, 0, cb0 + j)),
                  pl.BlockSpec((CONV_K, ct), lambda b, j: (0, j)),
                  pl.BlockSpec((1, ct), lambda b, j: (0, j))],
        out_specs=out_specs,
        out_shape=out_shape,
        scratch_shapes=[pltpu.VMEM((t + 3 * SUBLANES, ct), F32)],
        compiler_params=_cparams(("parallel", "arbitrary")),
        name="dwconv_silu",
    )(p, conv_w, conv_b.reshape(1, c))


def _mla_prep_kernel(cq_ref, ckv_ref, ra_ref, rb_ref, cos_ref, sin_ref, qnw_ref, kvnw_ref,
                     wq_ref, wqs_ref, wk_ref, wva_ref, wvb_ref, q_ref, k_ref, va_ref, vb_ref):
    cos = cos_ref[...]
    sin = sin_ref[...]
    cos8 = jnp.concatenate([cos] * MLA_HEADS, axis=1)
    sin8 = jnp.concatenate([sin] * MLA_HEADS, axis=1)
    cqn = _rms(cq_ref[...], qnw_ref[...]).astype(BF16)
    q = _dot(cqn, wq_ref[...]) * cos8 + _dot(cqn, wqs_ref[...]) * sin8
    q_ref[...] = q.astype(BF16)
    ckvn = _rms(ckv_ref[...], kvnw_ref[...]).astype(BF16)
    kr = ra_ref[...] * cos + rb_ref[...] * sin
    k = _dot(ckvn, wk_ref[...]) + jnp.concatenate([kr] * MLA_HEADS, axis=1)
    k_ref[...] = k.astype(BF16)
    lane = lax.broadcasted_iota(jnp.int32, (1, va_ref.shape[1]), 1) % (2 * MLA_V)
    va_ref[...] = (_dot(ckvn, wva_ref[...]) + (lane == ATT_SUM_LANE_A).astype(F32)).astype(BF16)
    vb_ref[...] = (_dot(ckvn, wvb_ref[...]) + (lane == ATT_SUM_LANE_B).astype(F32)).astype(BF16)


def _mla_prep(p, cos_t, sin_t, qnw, kvnw, wq, wqs, wk, wva, wvb, tm):
    bsz, t, _ = p.shape
    hq = MLA_HEADS * HEAD_PAD
    hv = MLA_HEADS * MLA_V
    full = lambda shape: pl.BlockSpec(shape, lambda b, i: (0, 0))
    return pl.pallas_call(
        _mla_prep_kernel,
        grid=(bsz, t // tm),
        in_specs=[pl.BlockSpec((None, tm, Q_LORA), lambda b, i: (b, i, EV_CQ // Q_LORA)),
                  pl.BlockSpec((None, tm, KV_LORA), lambda b, i: (b, i, EV_CKV // KV_LORA)),
                  pl.BlockSpec((None, tm, LANES), lambda b, i: (b, i, EV_ROPE_A // LANES)),
                  pl.BlockSpec((None, tm, LANES), lambda b, i: (b, i, EV_ROPE_B // LANES)),
                  pl.BlockSpec((tm, LANES), lambda b, i: (i, 0)),
                  pl.BlockSpec((tm, LANES), lambda b, i: (i, 0)),
                  full((1, Q_LORA)), full((1, KV_LORA)),
                  full((Q_LORA, hq)), full((Q_LORA, hq)), full((KV_LORA, hq)),
                  full((KV_LORA, hv)), full((KV_LORA, hv))],
        out_specs=[pl.BlockSpec((None, tm, hq), lambda b, i: (b, i, 0)),
                   pl.BlockSpec((None, tm, hq), lambda b, i: (b, i, 0)),
                   pl.BlockSpec((None, tm, hv), lambda b, i: (b, i, 0)),
                   pl.BlockSpec((None, tm, hv), lambda b, i: (b, i, 0))],
        out_shape=[jax.ShapeDtypeStruct((bsz, t, hq), BF16), jax.ShapeDtypeStruct((bsz, t, hq), BF16),
                   jax.ShapeDtypeStruct((bsz, t, hv), BF16), jax.ShapeDtypeStruct((bsz, t, hv), BF16)],
        compiler_params=_cparams(("parallel", "parallel")),
        name="mla_prep",
    )(p, p, p, p, cos_t, sin_t, qnw.reshape(1, -1), kvnw.reshape(1, -1), wq, wqs, wk, wva, wvb)


def _attn_kernel(q_ref, k_ref, va_ref, vb_ref, o_ref, *, lc, nct):
    i = pl.program_id(2)

    def attend(nk):
        lane = lax.broadcasted_iota(jnp.int32, (1, 2 * MLA_V), 1)
        acc = None
        for e, (v_ref, sum_lane) in enumerate(((va_ref, ATT_SUM_LANE_A), (vb_ref, ATT_SUM_LANE_B))):
            qh = q_ref[:, e * HEAD_PAD:(e + 1) * HEAD_PAD]
            kh = k_ref[0:nk, e * HEAD_PAD:(e + 1) * HEAD_PAD]
            s = _dot_nt(qh, kh)
            m = jnp.max(s, axis=-1, keepdims=True)
            p = jnp.exp2((s - m) * (MLA_SCALE * LOG2E))
            o = _dot(p.astype(BF16), v_ref[0:nk, :])
            l = jnp.sum(jnp.where(lane == sum_lane, o, 0.0), axis=-1, keepdims=True)
            o = jnp.where((lane // MLA_V) == e, o / l, 0.0)
            acc = o if acc is None else acc + o
        o_ref[...] = acc.astype(BF16)

    @pl.when(i < nct)
    def _():
        attend(lc)

    @pl.when(i >= nct)
    def _():
        attend(k_ref.shape[0])


def _attention(q, k, va, vb, tq, lc):
    bsz, t, _ = q.shape
    npair = MLA_HEADS // 2
    return pl.pallas_call(
        functools.partial(_attn_kernel, lc=lc, nct=lc // tq),
        grid=(bsz, npair, t // tq),
        in_specs=[pl.BlockSpec((None, tq, 2 * HEAD_PAD), lambda b, p, i: (b, i, p)),
                  pl.BlockSpec((None, t, 2 * HEAD_PAD), lambda b, p, i: (b, 0, p)),
                  pl.BlockSpec((None, t, 2 * MLA_V), lambda b, p, i: (b, 0, p)),
                  pl.BlockSpec((None, t, 2 * MLA_V), lambda b, p, i: (b, 0, p))],
        out_specs=pl.BlockSpec((None, tq, 2 * MLA_V), lambda b, p, i: (b, i, p)),
        out_shape=jax.ShapeDtypeStruct((bsz, t, MLA_HEADS * MLA_V), BF16),
        compiler_params=_cparams(("parallel", "parallel", "parallel")),
        name="mla_attention",
    )(q, k, va, vb)


def _chunk_of_step(s, n, ncc):
    r = s - n
    rev = jnp.where(r < ncc, ncc - 1 - r, n - 1 - (r - ncc))
    return jnp.where(s < n, s, rev)


def _scan_masks(rev):
    ti = lax.broadcasted_iota(jnp.int32, (CHUNK, CHUNK), 0)
    si = lax.broadcasted_iota(jnp.int32, (CHUNK, CHUNK), 1)
    eye = ti == si
    if rev:
        cum_m = (ti >= si).astype(BF16)
        mask = si >= ti
        last = si[0:1, :] == 0
    else:
        cum_m = (ti <= si).astype(BF16)
        mask = si <= ti
        last = si[0:1, :] == CHUNK - 1
    return cum_m, mask, eye, last


def _to_col(row, eye):
    return jnp.sum(jnp.where(eye, row, 0.0), axis=1, keepdims=True)


def _pick_last(row, last):
    return jnp.sum(jnp.where(last, row, 0.0), axis=1, keepdims=True)


def _ssd_kernel(xs_ref, bm_ref, cm_ref, z_ref, dt_ref, dtb_ref, alog_ref, dskip_ref, nw_ref,
                o_ref, ybuf, ht, *, n, ncc):
    s = pl.program_id(1)
    c = _chunk_of_step(s, n, ncc)
    hpg = SSM_HEADS // SSM_GROUPS
    gw = hpg * SSM_HEADDIM

    @pl.when(jnp.logical_or(s == 0, s == n))
    def _():
        ht[...] = jnp.zeros_like(ht)

    def chunk(rev):
        d = 1 if rev else 0
        cum_m, mask, eye, last = _scan_masks(rev)
        rows = slice(d * SSM_HEADS, (d + 1) * SSM_HEADS)
        dt = _softplus(dt_ref[rows, :] + dtb_ref[rows, :])
        da = dt * (-jnp.exp(alog_ref[rows, :]))
        cum = _dot_f32_rows(da, cum_m)
        head_of_lane = lax.broadcasted_iota(jnp.int32, (CHUNK, gw), 1) // SSM_HEADDIM
        xs = xs_ref[...]
        ys = []
        for g in range(SSM_GROUPS):
            bmb = bm_ref[:, g * SSM_STATE:(g + 1) * SSM_STATE].astype(BF16)
            cmb = cm_ref[:, g * SSM_STATE:(g + 1) * SSM_STATE].astype(BF16)
            cb = _dot_nt(cmb, bmb)
            xg = xs[:, g * gw:(g + 1) * gw]
            xgb = xg.astype(BF16)
            yg = jnp.zeros((CHUNK, gw), F32)
            eg = jnp.zeros((CHUNK, gw), F32)
            teg = jnp.zeros((CHUNK, gw), F32)
            dg = jnp.zeros((1, gw), F32)
            for r in range(hpg):
                h = g * hpg + r
                row = cum[h:h + 1, :]
                dtrow = dt[h:h + 1, :]
                col = _to_col(row, eye)
                dtcol = _to_col(dtrow, eye)
                tot = _pick_last(row, last)
                decay = jnp.exp(jnp.where(mask, col - row, -jnp.inf))
                w = cb * decay * dtrow
                yh = _dot(w.astype(BF16), xgb)
                sel = head_of_lane == r
                yg = jnp.where(sel, yh, yg)
                eg = jnp.where(sel, jnp.exp(col), eg)
                teg = jnp.where(sel, jnp.exp(tot - col) * dtcol, teg)
                dg = jnp.where(sel[0:1, :], jnp.exp(tot), dg)
            htg = ht[g]
            yg = yg + _dot(cmb, htg.astype(BF16)) * eg
            ht[g] = htg * dg + _dot_tn(bmb, (xg * teg).astype(BF16))
            ys.append(yg)
        return jnp.concatenate(ys, axis=1), xs

    @pl.when(s < n)
    def _():
        y, xs = chunk(False)
        ybuf[c] = y + dskip_ref[...] * xs

    @pl.when(s >= n)
    def _():
        y, _ = chunk(True)
        y = y + ybuf[c]
        z = z_ref[...]
        gated = y * (z * _sigmoid(z))
        outs = []
        for g in range(SSM_GROUPS):
            gg = gated[:, g * gw:(g + 1) * gw]
            outs.append(gg * lax.rsqrt(jnp.mean(gg * gg, axis=-1, keepdims=True) + EPS))
        o_ref[...] = (jnp.concatenate(outs, axis=1) * nw_ref[...]).astype(BF16)


def _ssd(cv, p, dt_t, dt_bias, a_log, d_skip, norm_w, lc):
    bsz, t, _ = cv.shape
    n = t // CHUNK
    ncc = lc // CHUNK
    cidx = lambda s: _chunk_of_step(s, n, ncc)
    cidx_out = lambda s: _chunk_of_step(jnp.maximum(s, n), n, ncc)
    nst = SSM_GROUPS * SSM_STATE
    small = lambda shape: pl.BlockSpec(shape, lambda b, s: (0, 0))
    return pl.pallas_call(
        functools.partial(_ssd_kernel, n=n, ncc=ncc),
        grid=(bsz, 2 * n),
        in_specs=[pl.BlockSpec((None, CHUNK, SSM_DINNER), lambda b, s: (b, cidx(s), 0)),
                  pl.BlockSpec((None, CHUNK, nst), lambda b, s: (b, cidx(s), SSM_DINNER // nst)),
                  pl.BlockSpec((None, CHUNK, nst), lambda b, s: (b, cidx(s), SSM_DINNER // nst + 1)),
                  pl.BlockSpec((None, CHUNK, SSM_DINNER), lambda b, s: (b, cidx_out(s), EV_Z // SSM_DINNER)),
                  pl.BlockSpec((None, 2 * SSM_HEADS, CHUNK), lambda b, s: (b, 0, cidx(s))),
                  small((2 * SSM_HEADS, 1)), small((2 * SSM_HEADS, 1)),
                  small((1, SSM_DINNER)), small((1, SSM_DINNER))],
        out_specs=pl.BlockSpec((None, CHUNK, SSM_DINNER), lambda b, s: (b, cidx_out(s), 0)),
        out_shape=jax.ShapeDtypeStruct((bsz, t, SSM_DINNER), BF16),
        scratch_shapes=[pltpu.VMEM((n, CHUNK, SSM_DINNER), F32),
                        pltpu.VMEM((SSM_GROUPS, SSM_STATE, SSM_DINNER // SSM_GROUPS), F32)],
        compiler_params=_cparams(("parallel", "arbitrary")),
        name="ssd_scan",
    )(cv, cv, cv, p, dt_t, dt_bias.reshape(-1, 1), a_log.reshape(-1, 1),
      jnp.repeat(d_skip, SSM_HEADDIM).reshape(1, -1), norm_w.reshape(1, -1))


MLSTM_XR = 16


def _mlstm_kernel(k_ref, qt_ref, vt_ref, o_ref, g_ref, ib_ref, fb_ref, nw_ref,
                  out_ref, hbuf, cst, mst, *, n, ncc):
    s = pl.program_id(1)
    c = _chunk_of_step(s, n, ncc)
    nh, dk, dv = MLSTM_HEADS, MLSTM_DQK, MLSTM_DV
    pw = 2 * dk

    @pl.when(jnp.logical_or(s == 0, s == n))
    def _():
        cst[...] = jnp.zeros_like(cst)
        mst[...] = jnp.zeros_like(mst)

    def chunk(rev):
        d = 1 if rev else 0
        si = lax.broadcasted_iota(jnp.int32, (CHUNK, CHUNK), 0)
        ti = lax.broadcasted_iota(jnp.int32, (CHUNK, CHUNK), 1)
        seen = (si >= ti) if rev else (si <= ti)
        cum_m = seen.astype(BF16)
        ipre = g_ref[d * nh:(d + 1) * nh, :] + ib_ref[d * nh:(d + 1) * nh, :]
        fpre = g_ref[(2 + d) * nh:(3 + d) * nh, :] + fb_ref[d * nh:(d + 1) * nh, :]
        bcum = _dot_f32_rows(-_softplus(-fpre), cum_m)
        lane = lax.broadcasted_iota(jnp.int32, (1, CHUNK), 1)
        btot = jnp.sum(jnp.where(lane == (0 if rev else CHUNK - 1), bcum, 0.0), axis=1, keepdims=True)
        cs = ipre - bcum
        wend = btot + cs
        m_in = mst[:, 0:1]
        m_new = jnp.maximum(btot + m_in, jnp.max(wend, axis=1, keepdims=True))
        keep = jnp.exp(btot + m_in - m_new)
        wt = jnp.exp(wend - m_new)
        mst[...] = jnp.broadcast_to(m_new, mst.shape)

        row_half = lax.broadcasted_iota(jnp.int32, (pw, CHUNK), 0) // dk
        lane_half = lax.broadcasted_iota(jnp.int32, (CHUNK, pw), 1) // dk
        ones_rows = (lax.broadcasted_iota(jnp.int32, (MLSTM_XR, CHUNK), 0) == 0).astype(F32)
        zblk = jnp.zeros((CHUNK, CHUNK), BF16)
        outs = []
        for p in range(nh // 2):
            kp = k_ref[:, p * pw:(p + 1) * pw] * (dk ** -0.5)
            qtp = qt_ref[p * pw:(p + 1) * pw, :]
            qbd = jnp.concatenate([jnp.where(row_half == 0, qtp, 0.0), jnp.where(row_half == 1, qtp, 0.0)],
                                  axis=1).astype(BF16)
            st = _dot(kp.astype(BF16), qbd)
            cp = cst[p]
            inter = _dot(cp.astype(BF16), qbd)
            ats, keepqs, mrows = [], [], []
            for e in range(2):
                h = 2 * p + e
                cs_st = jnp.transpose(jnp.broadcast_to(cs[h:h + 1, :], (CHUNK, CHUNK)))
                ex = jnp.where(seen, cs_st, -jnp.inf)
                mrow = jnp.maximum(m_in[h:h + 1, :], jnp.max(ex, axis=0, keepdims=True))
                ats.append((st[:, e * CHUNK:(e + 1) * CHUNK] * jnp.exp(ex - mrow)).astype(BF16))
                keepqs.append(jnp.exp(m_in[h:h + 1, :] - mrow))
                mrows.append(mrow)
            abd = jnp.concatenate([jnp.concatenate([ats[0], zblk], axis=1),
                                   jnp.concatenate([zblk, ats[1]], axis=1)], axis=0)
            vtp = jnp.concatenate(
                [jnp.concatenate([vt_ref[(2 * p + e) * dv:(2 * p + e + 1) * dv, :], ones_rows], axis=0)
                 for e in range(2)], axis=1)
            nd = _dot(vtp.astype(BF16), abd) + inter * jnp.concatenate(keepqs, axis=1)
            for e in range(2):
                h = 2 * p + e
                num = nd[0:dv, e * CHUNK:(e + 1) * CHUNK]
                den = nd[dv:dv + 1, e * CHUNK:(e + 1) * CHUNK]
                mt = bcum[h:h + 1, :] + mrows[e]
                outs.append(num / jnp.maximum(jnp.abs(den), jnp.exp(-mt)))
            wtp = jnp.concatenate([wt[2 * p:2 * p + 1, :], wt[2 * p + 1:2 * p + 2, :]], axis=1)
            kbd = jnp.concatenate([jnp.where(lane_half == 0, kp, 0.0), jnp.where(lane_half == 1, kp, 0.0)],
                                  axis=0).astype(BF16)
            keepp = jnp.where(lane_half[0:1, :] == 0, keep[2 * p:2 * p + 1, :], keep[2 * p + 1:2 * p + 2, :])
            cst[p] = cp * keepp + _dot((vtp * wtp).astype(BF16), kbd)
        return jnp.concatenate(outs, axis=0)

    @pl.when(s < n)
    def _():
        hbuf[c] = chunk(False)

    @pl.when(s >= n)
    def _():
        ht = chunk(True) + hbuf[c]
        outs = []
        for h in range(nh):
            hh = ht[h * dv:(h + 1) * dv, :]
            hn = hh * lax.rsqrt(jnp.mean(hh * hh, axis=0, keepdims=True) + EPS)
            outs.append(jnp.transpose(hn))
        hn = jnp.concatenate(outs, axis=1) * nw_ref[...]
        out_ref[...] = (_sigmoid(o_ref[...]) * hn).astype(BF16)


def _mlstm(cv, q_t, v_t, p, g_t, i_bias, f_bias, head_norm_w, lc):
    bsz, t, _ = cv.shape
    n = t // CHUNK
    ncc = lc // CHUNK
    cidx = lambda s: _chunk_of_step(s, n, ncc)
    cidx_out = lambda s: _chunk_of_step(jnp.maximum(s, n), n, ncc)
    nh = MLSTM_HEADS
    small = lambda shape: pl.BlockSpec(shape, lambda b, s: (0, 0))
    return pl.pallas_call(
        functools.partial(_mlstm_kernel, n=n, ncc=ncc),
        grid=(bsz, 2 * n),
        in_specs=[pl.BlockSpec((None, CHUNK, MLSTM_QK), lambda b, s: (b, cidx(s), 1)),
                  pl.BlockSpec((None, MLSTM_QK, CHUNK), lambda b, s: (b, 0, cidx(s))),
                  pl.BlockSpec((None, MLSTM_VW, CHUNK), lambda b, s: (b, 0, cidx(s))),
                  pl.BlockSpec((None, CHUNK, MLSTM_VW), lambda b, s: (b, cidx_out(s), OD_O // MLSTM_VW)),
                  pl.BlockSpec((None, 4 * nh, CHUNK), lambda b, s: (b, 0, cidx(s))),
                  small((2 * nh, 1)), small((2 * nh, 1)), small((1, MLSTM_VW))],
        out_specs=pl.BlockSpec((None, CHUNK, MLSTM_VW), lambda b, s: (b, cidx_out(s), 0)),
        out_shape=jax.ShapeDtypeStruct((bsz, t, MLSTM_VW), BF16),
        scratch_shapes=[pltpu.VMEM((n, MLSTM_VW, CHUNK), F32),
                        pltpu.VMEM((nh // 2, MLSTM_DV + MLSTM_XR, 2 * MLSTM_DQK), F32),
                        pltpu.VMEM((nh, LANES), F32)],
        compiler_params=_cparams(("parallel", "arbitrary")),
        name="mlstm_scan",
    )(cv, q_t, v_t, p, g_t, i_bias.reshape(-1, 1), f_bias.reshape(-1, 1), head_norm_w.reshape(1, -1))


def _mix_mlp_kernel(*refs, nin, tf):
    ms = refs[:nin]
    ws = refs[nin:2 * nin]
    x_ref, g1_ref, nw_ref, sh_ref, sc_ref, g2_ref, w1_ref, w2_ref, o_ref = refs[2 * nin:]
    mix = _dot(ms[0][...], ws[0][...])
    for m_ref, w_ref in zip(ms[1:], ws[1:]):
        mix = mix + _dot(m_ref[...], w_ref[...])
    x = x_ref[...] + g1_ref[...] * mix
    h = (_rms(x, nw_ref[...]) * (1.0 + sc_ref[...]) + sh_ref[...]).astype(BF16)
    acc = jnp.zeros(x.shape, F32)
    for f in range(0, w1_ref.shape[1], tf):
        u = jnp.maximum(_dot(h, w1_ref[:, f:f + tf]), 0.0)
        acc = acc + _dot((u * u).astype(BF16), w2_ref[f:f + tf, :])
    o_ref[...] = x + g2_ref[...] * acc


def _mix_mlp(ms, ws, xs, nw, mod, w1, w2, tm, nct):
    bsz, t, d = xs.shape
    dff = w1.shape[1]
    nin = len(ms)
    return pl.pallas_call(
        functools.partial(_mix_mlp_kernel, nin=nin, tf=1024),
        grid=(bsz, t // tm),
        in_specs=[pl.BlockSpec((None, tm, m.shape[2]), lambda b, i: (b, i, 0)) for m in ms]
        + [pl.BlockSpec(w.shape, lambda b, i: (0, 0)) for w in ws]
        + [pl.BlockSpec((None, tm, d), lambda b, i: (b, i, 0)), _mod_spec(d, 2, nct),
           pl.BlockSpec((1, d), lambda b, i: (0, 0)),
           _mod_spec(d, 3, nct), _mod_spec(d, 4, nct), _mod_spec(d, 5, nct),
           pl.BlockSpec((d, dff), lambda b, i: (0, 0)),
           pl.BlockSpec((dff, d), lambda b, i: (0, 0))],
        out_specs=pl.BlockSpec((None, tm, d), lambda b, i: (b, i, 0)),
        out_shape=jax.ShapeDtypeStruct((bsz, t, d), F32),
        compiler_params=_cparams(("parallel", "parallel")),
        name="outproj_mlp_residual",
    )(*ms, *ws, xs, mod, nw.reshape(1, d), mod, mod, mod, w1, w2)


def _final_kernel(x_ref, w_ref, o_ref):
    o_ref[...] = _rms(x_ref[...], w_ref[...])


def _final_norm(xs, w, tm, nct, l):
    bsz, _, d = xs.shape
    return pl.pallas_call(
        _final_kernel,
        grid=(bsz, l // tm),
        in_specs=[pl.BlockSpec((None, tm, d), lambda b, i: (b, i + nct, 0)),
                  pl.BlockSpec((1, d), lambda b, i: (0, 0))],
        out_specs=pl.BlockSpec((None, tm, d), lambda b, i: (b, i, 0)),
        out_shape=jax.ShapeDtypeStruct((bsz, l, d), F32),
        compiler_params=_cparams(("parallel", "parallel")),
        name="final_norm",
    )(xs, w.reshape(1, d))


def _rope_swap(w):
    j = np.arange(MLA_ROPE)
    first = (j % 16) < 8
    src = np.where(first, j + 8, j - 8)
    sign = np.where(first, -1.0, 1.0).astype(np.float32)
    return w[..., src] * sign


def _rope_tables(lc, l):
    n_rows = l // GRID_W
    row = jnp.repeat(jnp.arange(n_rows), GRID_W).astype(F32)
    col = jnp.tile(jnp.arange(GRID_W), n_rows).astype(F32)
    half = MLA_ROPE // 2
    inv = 1.0 / (ROPE_THETA ** (jnp.arange(0, half, 2, dtype=F32) / half))
    ar = row[:, None] * inv
    ac = col[:, None] * inv
    cos_l = jnp.concatenate([jnp.ones((l, MLA_NOPE), F32), jnp.cos(ar), jnp.cos(ar), jnp.cos(ac), jnp.cos(ac),
                             jnp.ones((l, HEAD_PAD - MLA_NOPE - MLA_ROPE), F32)], axis=1)
    sin_l = jnp.concatenate([jnp.zeros((l, MLA_NOPE), F32), jnp.sin(ar), jnp.sin(ar), jnp.sin(ac), jnp.sin(ac),
                             jnp.zeros((l, HEAD_PAD - MLA_NOPE - MLA_ROPE), F32)], axis=1)
    cos_t = jnp.concatenate([jnp.ones((lc, HEAD_PAD), F32), cos_l], axis=0)
    sin_t = jnp.concatenate([jnp.zeros((lc, HEAD_PAD), F32), sin_l], axis=0)
    return cos_t, sin_t


def _even_weights(w_in, w_uq, w_ukv, w_out):
    d = w_in.shape[0]
    o = np.cumsum([0, Q_LORA, KV_LORA, MLA_ROPE, SSM_DINNER, SSM_XBC, 2 * SSM_HEADS])
    cq, ckv, kr, z, xbc, dt = (w_in[:, o[i]:o[i + 1]] for i in range(6))
    zc = lambda n: jnp.zeros((d, n), F32)
    w_in_p = jnp.concatenate(
        [cq, ckv, z, xbc,
         zc(MLA_NOPE), kr, zc(HEAD_PAD - MLA_NOPE - MLA_ROPE),
         zc(MLA_NOPE), _rope_swap(kr), zc(HEAD_PAD - MLA_NOPE - MLA_ROPE),
         dt, zc(LANES - 2 * SSM_HEADS)], axis=1).astype(BF16)
    assert w_in_p.shape[1] == EV_N
    uq = w_uq.reshape(Q_LORA, MLA_HEADS, MLA_NOPE + MLA_ROPE)
    q_nope, q_rope = uq[..., :MLA_NOPE], uq[..., MLA_NOPE:]
    zq = lambda n: jnp.zeros((Q_LORA, MLA_HEADS, n), F32)
    npad = HEAD_PAD - MLA_NOPE - MLA_ROPE
    wq = jnp.concatenate([q_nope, q_rope, zq(npad)], axis=-1).reshape(Q_LORA, -1).astype(BF16)
    wqs = jnp.concatenate([zq(MLA_NOPE), _rope_swap(q_rope), zq(npad)], axis=-1).reshape(Q_LORA, -1).astype(BF16)
    ukv = w_ukv.reshape(KV_LORA, MLA_HEADS, MLA_NOPE + MLA_V)
    k_nope, vw = ukv[..., :MLA_NOPE], ukv[..., MLA_NOPE:]
    wk = jnp.concatenate([k_nope, jnp.zeros((KV_LORA, MLA_HEADS, HEAD_PAD - MLA_NOPE), F32)], axis=-1)
    wk = wk.reshape(KV_LORA, -1).astype(BF16)
    vpair = vw.reshape(KV_LORA, MLA_HEADS // 2, 2, MLA_V)
    zv = jnp.zeros((KV_LORA, MLA_HEADS // 2, MLA_V), F32)
    wva = jnp.concatenate([vpair[:, :, 0], zv], axis=-1).reshape(KV_LORA, -1).astype(BF16)
    wvb = jnp.concatenate([zv, vpair[:, :, 1]], axis=-1).reshape(KV_LORA, -1).astype(BF16)
    hv = MLA_HEADS * MLA_V
    return w_in_p, wq, wqs, wk, wva, wvb, w_out[:hv].astype(BF16), w_out[hv:].astype(BF16)


def _odd_weights(w_in):
    d = w_in.shape[0]
    ngate = 4 * MLSTM_HEADS
    w = jnp.concatenate([w_in, jnp.zeros((d, OD_N - OD_G - ngate), F32)], axis=1).astype(BF16)
    assert w.shape[1] == OD_N
    return w


def kernel(x, c, ctx, c_ctx, ada_w, ada_b, norm_mix_w, norm_mlp_w, mlp_w1, mlp_w2, ev_w_in, ev_q_norm_w, ev_w_uq, ev_kv_norm_w, ev_w_ukv, ev_conv_w, ev_conv_b, ev_dt_bias, ev_a_log, ev_d_skip, ev_ssm_norm_w, ev_w_out, od_w_in, od_conv_w, od_conv_b, od_i_bias, od_f_bias, od_head_norm_w, od_w_out, final_norm_w):
    bsz, l, d = x.shape
    lc = ctx.shape[1]
    depth = ada_w.shape[0]
    tm = 256 if lc % 256 == 0 else CHUNK
    assert lc % tm == 0 and l % tm == 0 and l % GRID_W == 0
    nct = lc // tm

    rows = -(-(bsz + 1) // SUBLANES) * SUBLANES
    svec = jnp.concatenate([c, c_ctx[None], jnp.zeros((rows - bsz - 1, d), F32)], axis=0)
    mods = _ada(svec, ada_w, ada_b)
    mod_ctx = jnp.broadcast_to(mods[:, bsz][:, None], (depth, bsz, 6 * d))
    mod_all = jnp.stack([mod_ctx, mods[:, :bsz]], axis=2)[:, :, :, None, :]

    cos_t, sin_t = _rope_tables(lc, l)
    xs = jnp.concatenate([ctx, x], axis=1)

    for layer in range(depth):
        mod = mod_all[layer]
        if layer % 2 == 0:
            e = layer // 2
            w_in_p, wq, wqs, wk, wva, wvb, wo_a, wo_s = _even_weights(ev_w_in[e], ev_w_uq[e], ev_w_ukv[e], ev_w_out[e])
            (p,) = _inproj(xs, norm_mix_w[layer], mod, w_in_p, tm, nct)
            q, k, va, vb = _mla_prep(p, cos_t, sin_t, ev_q_norm_w[e], ev_kv_norm_w[e], wq, wqs, wk, wva, wvb, tm)
            o_attn = _attention(q, k, va, vb, tm, lc)
            (cv,) = _conv(p, EV_XBC, ev_conv_w[e], ev_conv_b[e], lc, l)
            dt_t = jnp.swapaxes(p[:, :, EV_DT:EV_DT + 2 * SSM_HEADS], 1, 2)
            s_ssd = _ssd(cv, p, dt_t, ev_dt_bias[e], ev_a_log[e], ev_d_skip[e], ev_ssm_norm_w[e], lc)
            mix, mix_w = [o_attn, s_ssd], [wo_a, wo_s]
        else:
            o = layer // 2
            p, v_t = _inproj(xs, norm_mix_w[layer], mod, _odd_weights(od_w_in[o]), tm, nct,
                             tcols=(OD_V, MLSTM_VW))
            cv, q_t = _conv(p, OD_QK, od_conv_w[o], od_conv_b[o], lc, l, ntt=MLSTM_QK // 256)
            g_t = jnp.swapaxes(p[:, :, OD_G:OD_G + 4 * MLSTM_HEADS], 1, 2)
            m = _mlstm(cv, q_t, v_t, p, g_t, od_i_bias[o], od_f_bias[o], od_head_norm_w[o], lc)
            mix, mix_w = [m], [od_w_out[o].astype(BF16)]
        xs = _mix_mlp(mix, mix_w, xs, norm_mlp_w[layer], mod, mlp_w1[layer].astype(BF16),
                      mlp_w2[layer].astype(BF16), tm, nct)
    return _final_norm(xs, final_norm_w, tm, nct, l)
```

```python
import functools

import jax
import jax.numpy as jnp
import numpy as np
from jax import lax
from jax.experimental import pallas as pl
from jax.experimental.pallas import tpu as pltpu

F32 = jnp.float32
BF16 = jnp.bfloat16

EPS = 1e-6
GRID_W = 64
CHUNK = 128
CONV_K = 5
LANES = 128
SUBLANES = 8

MLA_HEADS = 8
MLA_NOPE = 64
MLA_ROPE = 32
MLA_V = 64
Q_LORA = 256
KV_LORA = 256
ROPE_THETA = 10000.0
MLA_SCALE = (MLA_NOPE + MLA_ROPE) ** -0.5
HEAD_PAD = 128
LOG2E = 1.4426950408889634
ATT_SUM_LANE_A = MLA_V
ATT_SUM_LANE_B = 0

SSM_HEADS = 8
SSM_HEADDIM = 64
SSM_DINNER = SSM_HEADS * SSM_HEADDIM
SSM_GROUPS = 2
SSM_STATE = 128
SSM_XBC = SSM_DINNER + 2 * SSM_GROUPS * SSM_STATE

MLSTM_HEADS = 8
MLSTM_DQK = 64
MLSTM_DV = 128
MLSTM_QK = MLSTM_HEADS * MLSTM_DQK
MLSTM_VW = MLSTM_HEADS * MLSTM_DV

VMEM_LIMIT = 56 * 1024 * 1024

EV_CQ, EV_CKV, EV_Z, EV_XBC = 0, 256, 512, 1024
EV_ROPE_A, EV_ROPE_B, EV_DT, EV_N = 2048, 2176, 2304, 2432
OD_QK, OD_V, OD_O, OD_G, OD_N = 0, 1024, 2048, 3072, 3200


def _cparams(sem):
    return pltpu.CompilerParams(dimension_semantics=sem, vmem_limit_bytes=VMEM_LIMIT)


def _dot(a, b):
    return jnp.dot(a, b, preferred_element_type=F32)


def _dot_nt(a, b):
    return lax.dot_general(a, b, (((1,), (1,)), ((), ())), preferred_element_type=F32)


def _dot_tn(a, b):
    return lax.dot_general(a, b, (((0,), (0,)), ((), ())), preferred_element_type=F32)


def _sigmoid(x):
    return 1.0 / (1.0 + jnp.exp(-x))


def _softplus(x):
    return jnp.maximum(x, 0.0) + jnp.log(1.0 + jnp.exp(-jnp.abs(x)))


def _rms(x, w):
    ms = jnp.mean(x * x, axis=-1, keepdims=True)
    return x * lax.rsqrt(ms + EPS) * w


def _dot_f32_rows(x, m_bf16):
    hi = x.astype(BF16)
    r1 = x - hi.astype(F32)
    mid = r1.astype(BF16)
    lo = (r1 - mid.astype(F32)).astype(BF16)
    return _dot(hi, m_bf16) + _dot(mid, m_bf16) + _dot(lo, m_bf16)


def _ada_kernel(s_ref, w_ref, b_ref, o_ref):
    s = s_ref[...]
    s = (s * _sigmoid(s)).astype(BF16)
    o_ref[...] = _dot(s, w_ref[...].astype(BF16)) + b_ref[...]


def _ada(svec, ada_w, ada_b):
    depth, d, n6 = ada_w.shape
    rows = svec.shape[0]
    tn = 1024
    return pl.pallas_call(
        _ada_kernel,
        grid=(depth, n6 // tn),
        in_specs=[pl.BlockSpec((rows, d), lambda l, j: (0, 0)),
                  pl.BlockSpec((None, d, tn), lambda l, j: (l, 0, j)),
                  pl.BlockSpec((None, 1, tn), lambda l, j: (l, 0, j))],
        out_specs=pl.BlockSpec((None, rows, tn), lambda l, j: (l, 0, j)),
        out_shape=jax.ShapeDtypeStruct((depth, rows, n6), F32),
        compiler_params=_cparams(("parallel", "parallel")),
        name="ada",
    )(svec, ada_w, ada_b.reshape(depth, 1, n6))


def _inproj_kernel(x_ref, nw_ref, sh_ref, sc_ref, w_ref, o_ref, *t_refs, tcols):
    h = _rms(x_ref[...], nw_ref[...]) * (1.0 + sc_ref[...]) + sh_ref[...]
    res = _dot(h.astype(BF16), w_ref[...])
    o_ref[...] = res
    if tcols is not None:
        t_refs[0][...] = jnp.transpose(res[:, tcols[0]:tcols[0] + tcols[1]])


def _mod_spec(d, chunk, nct):
    return pl.BlockSpec((None, None, 1, d), lambda b, i: (b, jnp.where(i >= nct, 1, 0), 0, chunk))


def _inproj(xs, nw, mod, w, tm, nct, tcols=None):
    bsz, t, d = xs.shape
    n = w.shape[1]
    out_specs = [pl.BlockSpec((None, tm, n), lambda b, i: (b, i, 0))]
    out_shape = [jax.ShapeDtypeStruct((bsz, t, n), F32)]
    if tcols is not None:
        out_specs.append(pl.BlockSpec((None, tcols[1], tm), lambda b, i: (b, 0, i)))
        out_shape.append(jax.ShapeDtypeStruct((bsz, tcols[1], t), F32))
    return pl.pallas_call(
        functools.partial(_inproj_kernel, tcols=tcols),
        grid=(bsz, t // tm),
        in_specs=[pl.BlockSpec((None, tm, d), lambda b, i: (b, i, 0)),
                  pl.BlockSpec((1, d), lambda b, i: (0, 0)),
                  _mod_spec(d, 0, nct), _mod_spec(d, 1, nct),
                  pl.BlockSpec((d, n), lambda b, i: (0, 0))],
        out_specs=out_specs,
        out_shape=out_shape,
        compiler_params=_cparams(("parallel", "parallel")),
        name="inproj",
    )(xs, nw.reshape(1, d), mod, mod, w)


def _conv_kernel(x_ref, w_ref, b_ref, o_ref, pad_ref, *, lc, l):
    ct = x_ref.shape[1]
    zeros = jnp.zeros((SUBLANES, ct), F32)
    pad_ref[0:SUBLANES, :] = zeros
    pad_ref[SUBLANES:SUBLANES + lc, :] = x_ref[0:lc, :]
    pad_ref[SUBLANES + lc:2 * SUBLANES + lc, :] = zeros
    pad_ref[2 * SUBLANES + lc:2 * SUBLANES + lc + l, :] = x_ref[lc:lc + l, :]
    pad_ref[2 * SUBLANES + lc + l:3 * SUBLANES + lc + l, :] = zeros
    w = w_ref[...]
    bias = b_ref[...]
    for r0 in range(0, lc + l, CHUNK):
        base = (SUBLANES if r0 < lc else 2 * SUBLANES) + r0
        acc = bias + w[0:1, :] * pad_ref[base - 2:base - 2 + CHUNK, :]
        for k in range(1, CONV_K):
            acc = acc + w[k:k + 1, :] * pad_ref[base + k - 2:base + k - 2 + CHUNK, :]
        o_ref[r0:r0 + CHUNK, :] = acc * _sigmoid(acc)


def _conv(p, col0, conv_w, conv_b, lc, l):
    bsz, t, _ = p.shape
    c = conv_w.shape[1]
    ct = 256
    cb0 = col0 // ct
    return pl.pallas_call(
        functools.partial(_conv_kernel, lc=lc, l=l),
        grid=(bsz, c // ct),
        in_specs=[pl.BlockSpec((None, t, ct), lambda b, j: (b, 0, cb0 + j)),
                  pl.BlockSpec((CONV_K, ct), lambda b, j: (0, j)),
                  pl.BlockSpec((1, ct), lambda b, j: (0, j))],
        out_specs=pl.BlockSpec((None, t, ct), lambda b, j: (b, 0, j)),
        out_shape=jax.ShapeDtypeStruct((bsz, t, c), F32),
        scratch_shapes=[pltpu.VMEM((t + 3 * SUBLANES, ct), F32)],
        compiler_params=_cparams(("parallel", "parallel")),
        name="dwconv_silu",
    )(p, conv_w, conv_b.reshape(1, c))


def _mla_prep_kernel(cq_ref, ckv_ref, ra_ref, rb_ref, cos_ref, sin_ref, qnw_ref, kvnw_ref,
                     wq_ref, wqs_ref, wk_ref, wva_ref, wvb_ref, q_ref, k_ref, va_ref, vb_ref):
    cos = cos_ref[...]
    sin = sin_ref[...]
    cos8 = jnp.concatenate([cos] * MLA_HEADS, axis=1)
    sin8 = jnp.concatenate([sin] * MLA_HEADS, axis=1)
    cqn = _rms(cq_ref[...], qnw_ref[...]).astype(BF16)
    q = _dot(cqn, wq_ref[...]) * cos8 + _dot(cqn, wqs_ref[...]) * sin8
    q_ref[...] = q.astype(BF16)
    ckvn = _rms(ckv_ref[...], kvnw_ref[...]).astype(BF16)
    kr = ra_ref[...] * cos + rb_ref[...] * sin
    k = _dot(ckvn, wk_ref[...]) + jnp.concatenate([kr] * MLA_HEADS, axis=1)
    k_ref[...] = k.astype(BF16)
    lane = lax.broadcasted_iota(jnp.int32, (1, va_ref.shape[1]), 1) % (2 * MLA_V)
    va_ref[...] = (_dot(ckvn, wva_ref[...]) + (lane == ATT_SUM_LANE_A).astype(F32)).astype(BF16)
    vb_ref[...] = (_dot(ckvn, wvb_ref[...]) + (lane == ATT_SUM_LANE_B).astype(F32)).astype(BF16)


def _mla_prep(p, cos_t, sin_t, qnw, kvnw, wq, wqs, wk, wva, wvb, tm):
    bsz, t, _ = p.shape
    hq = MLA_HEADS * HEAD_PAD
    hv = MLA_HEADS * MLA_V
    full = lambda shape: pl.BlockSpec(shape, lambda b, i: (0, 0))
    return pl.pallas_call(
        _mla_prep_kernel,
        grid=(bsz, t // tm),
        in_specs=[pl.BlockSpec((None, tm, Q_LORA), lambda b, i: (b, i, EV_CQ // Q_LORA)),
                  pl.BlockSpec((None, tm, KV_LORA), lambda b, i: (b, i, EV_CKV // KV_LORA)),
                  pl.BlockSpec((None, tm, LANES), lambda b, i: (b, i, EV_ROPE_A // LANES)),
                  pl.BlockSpec((None, tm, LANES), lambda b, i: (b, i, EV_ROPE_B // LANES)),
                  pl.BlockSpec((tm, LANES), lambda b, i: (i, 0)),
                  pl.BlockSpec((tm, LANES), lambda b, i: (i, 0)),
                  full((1, Q_LORA)), full((1, KV_LORA)),
                  full((Q_LORA, hq)), full((Q_LORA, hq)), full((KV_LORA, hq)),
                  full((KV_LORA, hv)), full((KV_LORA, hv))],
        out_specs=[pl.BlockSpec((None, tm, hq), lambda b, i: (b, i, 0)),
                   pl.BlockSpec((None, tm, hq), lambda b, i: (b, i, 0)),
                   pl.BlockSpec((None, tm, hv), lambda b, i: (b, i, 0)),
                   pl.BlockSpec((None, tm, hv), lambda b, i: (b, i, 0))],
        out_shape=[jax.ShapeDtypeStruct((bsz, t, hq), BF16), jax.ShapeDtypeStruct((bsz, t, hq), BF16),
                   jax.ShapeDtypeStruct((bsz, t, hv), BF16), jax.ShapeDtypeStruct((bsz, t, hv), BF16)],
        compiler_params=_cparams(("parallel", "parallel")),
        name="mla_prep",
    )(p, p, p, p, cos_t, sin_t, qnw.reshape(1, -1), kvnw.reshape(1, -1), wq, wqs, wk, wva, wvb)


def _attn_kernel(q_ref, k_ref, va_ref, vb_ref, o_ref, *, lc, nct):
    i = pl.program_id(2)

    def attend(nk):
        lane = lax.broadcasted_iota(jnp.int32, (1, 2 * MLA_V), 1)
        acc = None
        scores = [_dot_nt(q_ref[:, e * HEAD_PAD:(e + 1) * HEAD_PAD], k_ref[0:nk, e * HEAD_PAD:(e + 1) * HEAD_PAD])
                  for e in range(2)]
        for e, (v_ref, sum_lane) in enumerate(((va_ref, ATT_SUM_LANE_A), (vb_ref, ATT_SUM_LANE_B))):
            s = scores[e]
            m = jnp.max(s, axis=-1, keepdims=True)
            p = jnp.exp2((s - m) * (MLA_SCALE * LOG2E))
            o = _dot(p.astype(BF16), v_ref[0:nk, :])
            l = jnp.sum(jnp.where(lane == sum_lane, o, 0.0), axis=-1, keepdims=True)
            o = jnp.where((lane // MLA_V) == e, o / l, 0.0)
            acc = o if acc is None else acc + o
        o_ref[...] = acc.astype(BF16)

    @pl.when(i < nct)
    def _():
        attend(lc)

    @pl.when(i >= nct)
    def _():
        attend(k_ref.shape[0])


def _attention(q, k, va, vb, tq, lc):
    bsz, t, _ = q.shape
    npair = MLA_HEADS // 2
    return pl.pallas_call(
        functools.partial(_attn_kernel, lc=lc, nct=lc // tq),
        grid=(bsz, npair, t // tq),
        in_specs=[pl.BlockSpec((None, tq, 2 * HEAD_PAD), lambda b, p, i: (b, i, p)),
                  pl.BlockSpec((None, t, 2 * HEAD_PAD), lambda b, p, i: (b, 0, p)),
                  pl.BlockSpec((None, t, 2 * MLA_V), lambda b, p, i: (b, 0, p)),
                  pl.BlockSpec((None, t, 2 * MLA_V), lambda b, p, i: (b, 0, p))],
        out_specs=pl.BlockSpec((None, tq, 2 * MLA_V), lambda b, p, i: (b, i, p)),
        out_shape=jax.ShapeDtypeStruct((bsz, t, MLA_HEADS * MLA_V), BF16),
        compiler_params=_cparams(("parallel", "parallel", "parallel")),
        name="mla_attention",
    )(q, k, va, vb)


def _chunk_of_step(s, n, ncc):
    r = s - n
    rev = jnp.where(r < ncc, ncc - 1 - r, n - 1 - (r - ncc))
    return jnp.where(s < n, s, rev)


def _scan_masks(rev):
    ti = lax.broadcasted_iota(jnp.int32, (CHUNK, CHUNK), 0)
    si = lax.broadcasted_iota(jnp.int32, (CHUNK, CHUNK), 1)
    eye = ti == si
    if rev:
        cum_m = (ti >= si).astype(BF16)
        mask = si >= ti
        last = si[0:1, :] == 0
    else:
        cum_m = (ti <= si).astype(BF16)
        mask = si <= ti
        last = si[0:1, :] == CHUNK - 1
    return cum_m, mask, eye, last


def _to_col(row, eye):
    return jnp.sum(jnp.where(eye, row, 0.0), axis=1, keepdims=True)


def _pick_last(row, last):
    return jnp.sum(jnp.where(last, row, 0.0), axis=1, keepdims=True)


def _ssd_kernel(xs_ref, bm_ref, cm_ref, z_ref, dt_ref, dtb_ref, alog_ref, dskip_ref, nw_ref,
                o_ref, ybuf, ht, *, n, ncc):
    s = pl.program_id(1)
    c = _chunk_of_step(s, n, ncc)
    hpg = SSM_HEADS // SSM_GROUPS
    gw = hpg * SSM_HEADDIM

    @pl.when(jnp.logical_or(s == 0, s == n))
    def _():
        ht[...] = jnp.zeros_like(ht)

    def chunk(rev):
        d = 1 if rev else 0
        cum_m, mask, eye, last = _scan_masks(rev)
        rows = slice(d * SSM_HEADS, (d + 1) * SSM_HEADS)
        dt_rows = jnp.transpose(dt_ref[...])
        dt = _softplus(dt_rows[rows, :] + dtb_ref[rows, :])
        da = dt * (-jnp.exp(alog_ref[rows, :]))
        cum = _dot_f32_rows(da, cum_m)
        head_of_lane = lax.broadcasted_iota(jnp.int32, (CHUNK, gw), 1) // SSM_HEADDIM
        xs = xs_ref[...]
        ys = []
        for g in range(SSM_GROUPS):
            bmb = bm_ref[:, g * SSM_STATE:(g + 1) * SSM_STATE].astype(BF16)
            cmb = cm_ref[:, g * SSM_STATE:(g + 1) * SSM_STATE].astype(BF16)
            cb = _dot_nt(cmb, bmb)
            xg = xs[:, g * gw:(g + 1) * gw]
            xgb = xg.astype(BF16)
            yg = jnp.zeros((CHUNK, gw), F32)
            eg = jnp.zeros((CHUNK, gw), F32)
            teg = jnp.zeros((CHUNK, gw), F32)
            dg = jnp.zeros((1, gw), F32)
            for r in range(hpg):
                h = g * hpg + r
                row = cum[h:h + 1, :]
                dtrow = dt[h:h + 1, :]
                col = _to_col(row, eye)
                dtcol = _to_col(dtrow, eye)
                tot = _pick_last(row, last)
                decay = jnp.exp(jnp.where(mask, col - row, -jnp.inf))
                w = cb * decay * dtrow
                yh = _dot(w.astype(BF16), xgb)
                sel = head_of_lane == r
                yg = jnp.where(sel, yh, yg)
                eg = jnp.where(sel, jnp.exp(col), eg)
                teg = jnp.where(sel, jnp.exp(tot - col) * dtcol, teg)
                dg = jnp.where(sel[0:1, :], jnp.exp(tot), dg)
            htg = ht[g]
            yg = yg + _dot(cmb, htg.astype(BF16)) * eg
            ht[g] = htg * dg + _dot_tn(bmb, (xg * teg).astype(BF16))
            ys.append(yg)
        return jnp.concatenate(ys, axis=1), xs

    @pl.when(s < n)
    def _():
        y, xs = chunk(False)
        ybuf[c] = y + dskip_ref[...] * xs

    @pl.when(s >= n)
    def _():
        y, _ = chunk(True)
        y = y + ybuf[c]
        z = z_ref[...]
        gated = y * (z * _sigmoid(z))
        outs = []
        for g in range(SSM_GROUPS):
            gg = gated[:, g * gw:(g + 1) * gw]
            outs.append(gg * lax.rsqrt(jnp.mean(gg * gg, axis=-1, keepdims=True) + EPS))
        o_ref[...] = (jnp.concatenate(outs, axis=1) * nw_ref[...]).astype(BF16)


def _ssd(cv, p, dt_bias, a_log, d_skip, norm_w, lc):
    bsz, t, _ = cv.shape
    n = t // CHUNK
    ncc = lc // CHUNK
    cidx = lambda s: _chunk_of_step(s, n, ncc)
    cidx_out = lambda s: _chunk_of_step(jnp.maximum(s, n), n, ncc)
    nst = SSM_GROUPS * SSM_STATE
    small = lambda shape: pl.BlockSpec(shape, lambda b, s: (0, 0))
    return pl.pallas_call(
        functools.partial(_ssd_kernel, n=n, ncc=ncc),
        grid=(bsz, 2 * n),
        in_specs=[pl.BlockSpec((None, CHUNK, SSM_DINNER), lambda b, s: (b, cidx(s), 0)),
                  pl.BlockSpec((None, CHUNK, nst), lambda b, s: (b, cidx(s), SSM_DINNER // nst)),
                  pl.BlockSpec((None, CHUNK, nst), lambda b, s: (b, cidx(s), SSM_DINNER // nst + 1)),
                  pl.BlockSpec((None, CHUNK, SSM_DINNER), lambda b, s: (b, cidx_out(s), EV_Z // SSM_DINNER)),
                  pl.BlockSpec((None, CHUNK, LANES), lambda b, s: (b, cidx(s), EV_DT // LANES)),
                  small((2 * SSM_HEADS, 1)), small((2 * SSM_HEADS, 1)),
                  small((1, SSM_DINNER)), small((1, SSM_DINNER))],
        out_specs=pl.BlockSpec((None, CHUNK, SSM_DINNER), lambda b, s: (b, cidx_out(s), 0)),
        out_shape=jax.ShapeDtypeStruct((bsz, t, SSM_DINNER), BF16),
        scratch_shapes=[pltpu.VMEM((n, CHUNK, SSM_DINNER), F32),
                        pltpu.VMEM((SSM_GROUPS, SSM_STATE, SSM_DINNER // SSM_GROUPS), F32)],
        compiler_params=_cparams(("parallel", "arbitrary")),
        name="ssd_scan",
    )(cv, cv, cv, p, p, dt_bias.reshape(-1, 1), a_log.reshape(-1, 1),
      jnp.repeat(d_skip, SSM_HEADDIM).reshape(1, -1), norm_w.reshape(1, -1))


MLSTM_XR = 16


def _mlstm_kernel(k_ref, qt_ref, vt_ref, o_ref, g_ref, ib_ref, fb_ref, nw_ref,
                  out_ref, hbuf, cst, mst, *, n, ncc):
    s = pl.program_id(1)
    c = _chunk_of_step(s, n, ncc)
    nh, dk, dv = MLSTM_HEADS, MLSTM_DQK, MLSTM_DV
    pw = 2 * dk

    @pl.when(jnp.logical_or(s == 0, s == n))
    def _():
        cst[...] = jnp.zeros_like(cst)
        mst[...] = jnp.zeros_like(mst)

    def chunk(rev):
        d = 1 if rev else 0
        si = lax.broadcasted_iota(jnp.int32, (CHUNK, CHUNK), 0)
        ti = lax.broadcasted_iota(jnp.int32, (CHUNK, CHUNK), 1)
        seen = (si >= ti) if rev else (si <= ti)
        cum_m = seen.astype(BF16)
        g_rows = jnp.transpose(g_ref[...])
        ipre = g_rows[d * nh:(d + 1) * nh, :] + ib_ref[d * nh:(d + 1) * nh, :]
        fpre = g_rows[(2 + d) * nh:(3 + d) * nh, :] + fb_ref[d * nh:(d + 1) * nh, :]
        bcum = _dot_f32_rows(-_softplus(-fpre), cum_m)
        lane = lax.broadcasted_iota(jnp.int32, (1, CHUNK), 1)
        btot = jnp.sum(jnp.where(lane == (0 if rev else CHUNK - 1), bcum, 0.0), axis=1, keepdims=True)
        cs = ipre - bcum
        wend = btot + cs
        m_in = mst[:, 0:1]
        m_new = jnp.maximum(btot + m_in, jnp.max(wend, axis=1, keepdims=True))
        keep = jnp.exp(btot + m_in - m_new)
        wt = jnp.exp(wend - m_new)
        mst[...] = jnp.broadcast_to(m_new, mst.shape)

        row_half = lax.broadcasted_iota(jnp.int32, (pw, CHUNK), 0) // dk
        lane_half = lax.broadcasted_iota(jnp.int32, (CHUNK, pw), 1) // dk
        ones_rows = (lax.broadcasted_iota(jnp.int32, (MLSTM_XR, CHUNK), 0) == 0).astype(F32)
        zblk = jnp.zeros((CHUNK, CHUNK), BF16)
        outs = []
        for p in range(nh // 2):
            kp = k_ref[:, p * pw:(p + 1) * pw] * (dk ** -0.5)
            qtp = qt_ref[p * pw:(p + 1) * pw, :]
            qbd = jnp.concatenate([jnp.where(row_half == 0, qtp, 0.0), jnp.where(row_half == 1, qtp, 0.0)],
                                  axis=1).astype(BF16)
            st = _dot(kp.astype(BF16), qbd)
            cp = cst[p]
            inter = _dot(cp.astype(BF16), qbd)
            ats, keepqs, mrows = [], [], []
            for e in range(2):
                h = 2 * p + e
                cs_st = jnp.transpose(jnp.broadcast_to(cs[h:h + 1, :], (CHUNK, CHUNK)))
                ex = jnp.where(seen, cs_st, -jnp.inf)
                mrow = jnp.maximum(m_in[h:h + 1, :], jnp.max(ex, axis=0, keepdims=True))
                ats.append((st[:, e * CHUNK:(e + 1) * CHUNK] * jnp.exp(ex - mrow)).astype(BF16))
                keepqs.append(jnp.exp(m_in[h:h + 1, :] - mrow))
                mrows.append(mrow)
            abd = jnp.concatenate([jnp.concatenate([ats[0], zblk], axis=1),
                                   jnp.concatenate([zblk, ats[1]], axis=1)], axis=0)
            vtp = jnp.concatenate(
                [jnp.concatenate([vt_ref[(2 * p + e) * dv:(2 * p + e + 1) * dv, :], ones_rows], axis=0)
                 for e in range(2)], axis=1)
            nd = _dot(vtp.astype(BF16), abd) + inter * jnp.concatenate(keepqs, axis=1)
            for e in range(2):
                h = 2 * p + e
                num = nd[0:dv, e * CHUNK:(e + 1) * CHUNK]
                den = nd[dv:dv + 1, e * CHUNK:(e + 1) * CHUNK]
                mt = bcum[h:h + 1, :] + mrows[e]
                outs.append(num / jnp.maximum(jnp.abs(den), jnp.exp(-mt)))
            wtp = jnp.concatenate([wt[2 * p:2 * p + 1, :], wt[2 * p + 1:2 * p + 2, :]], axis=1)
            kbd = jnp.concatenate([jnp.where(lane_half == 0, kp, 0.0), jnp.where(lane_half == 1, kp, 0.0)],
                                  axis=0).astype(BF16)
            keepp = jnp.where(lane_half[0:1, :] == 0, keep[2 * p:2 * p + 1, :], keep[2 * p + 1:2 * p + 2, :])
            cst[p] = cp * keepp + _dot((vtp * wtp).astype(BF16), kbd)
        return jnp.concatenate(outs, axis=0)

    @pl.when(s < n)
    def _():
        hbuf[c] = chunk(False)

    @pl.when(s >= n)
    def _():
        ht = chunk(True) + hbuf[c]
        outs = []
        for h in range(nh):
            hh = ht[h * dv:(h + 1) * dv, :]
            hn = hh * lax.rsqrt(jnp.mean(hh * hh, axis=0, keepdims=True) + EPS)
            outs.append(jnp.transpose(hn))
        hn = jnp.concatenate(outs, axis=1) * nw_ref[...]
        out_ref[...] = (_sigmoid(o_ref[...]) * hn).astype(BF16)


def _mlstm(cv, q_t, v_t, p, i_bias, f_bias, head_norm_w, lc):
    bsz, t, _ = cv.shape
    n = t // CHUNK
    ncc = lc // CHUNK
    cidx = lambda s: _chunk_of_step(s, n, ncc)
    cidx_out = lambda s: _chunk_of_step(jnp.maximum(s, n), n, ncc)
    nh = MLSTM_HEADS
    small = lambda shape: pl.BlockSpec(shape, lambda b, s: (0, 0))
    return pl.pallas_call(
        functools.partial(_mlstm_kernel, n=n, ncc=ncc),
        grid=(bsz, 2 * n),
        in_specs=[pl.BlockSpec((None, CHUNK, MLSTM_QK), lambda b, s: (b, cidx(s), 1)),
                  pl.BlockSpec((None, MLSTM_QK, CHUNK), lambda b, s: (b, 0, cidx(s))),
                  pl.BlockSpec((None, MLSTM_VW, CHUNK), lambda b, s: (b, 0, cidx(s))),
                  pl.BlockSpec((None, CHUNK, MLSTM_VW), lambda b, s: (b, cidx_out(s), OD_O // MLSTM_VW)),
                  pl.BlockSpec((None, CHUNK, LANES), lambda b, s: (b, cidx(s), OD_G // LANES)),
                  small((2 * nh, 1)), small((2 * nh, 1)), small((1, MLSTM_VW))],
        out_specs=pl.BlockSpec((None, CHUNK, MLSTM_VW), lambda b, s: (b, cidx_out(s), 0)),
        out_shape=jax.ShapeDtypeStruct((bsz, t, MLSTM_VW), BF16),
        scratch_shapes=[pltpu.VMEM((n, MLSTM_VW, CHUNK), F32),
                        pltpu.VMEM((nh // 2, MLSTM_DV + MLSTM_XR, 2 * MLSTM_DQK), F32),
                        pltpu.VMEM((nh, LANES), F32)],
        compiler_params=_cparams(("parallel", "arbitrary")),
        name="mlstm_scan",
    )(cv, q_t, v_t, p, p, i_bias.reshape(-1, 1), f_bias.reshape(-1, 1), head_norm_w.reshape(1, -1))


def _mix_mlp_kernel(*refs, nin, tf):
    ms = refs[:nin]
    ws = refs[nin:2 * nin]
    x_ref, g1_ref, nw_ref, sh_ref, sc_ref, g2_ref, w1_ref, w2_ref, o_ref = refs[2 * nin:]
    mix = _dot(ms[0][...], ws[0][...])
    for m_ref, w_ref in zip(ms[1:], ws[1:]):
        mix = mix + _dot(m_ref[...], w_ref[...])
    x = x_ref[...] + g1_ref[...] * mix
    h = (_rms(x, nw_ref[...]) * (1.0 + sc_ref[...]) + sh_ref[...]).astype(BF16)
    acc = jnp.zeros(x.shape, F32)
    for f in range(0, w1_ref.shape[1], tf):
        u = jnp.maximum(_dot(h, w1_ref[:, f:f + tf]), 0.0)
        acc = acc + _dot((u * u).astype(BF16), w2_ref[f:f + tf, :])
    o_ref[...] = x + g2_ref[...] * acc


def _mix_mlp(ms, ws, xs, nw, mod, w1, w2, tm, nct):
    bsz, t, d = xs.shape
    dff = w1.shape[1]
    nin = len(ms)
    return pl.pallas_call(
        functools.partial(_mix_mlp_kernel, nin=nin, tf=1024),
        grid=(bsz, t // tm),
        in_specs=[pl.BlockSpec((None, tm, m.shape[2]), lambda b, i: (b, i, 0)) for m in ms]
        + [pl.BlockSpec(w.shape, lambda b, i: (0, 0)) for w in ws]
        + [pl.BlockSpec((None, tm, d), lambda b, i: (b, i, 0)), _mod_spec(d, 2, nct),
           pl.BlockSpec((1, d), lambda b, i: (0, 0)),
           _mod_spec(d, 3, nct), _mod_spec(d, 4, nct), _mod_spec(d, 5, nct),
           pl.BlockSpec((d, dff), lambda b, i: (0, 0)),
           pl.BlockSpec((dff, d), lambda b, i: (0, 0))],
        out_specs=pl.BlockSpec((None, tm, d), lambda b, i: (b, i, 0)),
        out_shape=jax.ShapeDtypeStruct((bsz, t, d), F32),
        compiler_params=_cparams(("parallel", "parallel")),
        name="outproj_mlp_residual",
    )(*ms, *ws, xs, mod, nw.reshape(1, d), mod, mod, mod, w1, w2)


def _final_kernel(x_ref, w_ref, o_ref):
    o_ref[...] = _rms(x_ref[...], w_ref[...])


def _final_norm(xs, w, tm, nct, l):
    bsz, _, d = xs.shape
    return pl.pallas_call(
        _final_kernel,
        grid=(bsz, l // tm),
        in_specs=[pl.BlockSpec((None, tm, d), lambda b, i: (b, i + nct, 0)),
                  pl.BlockSpec((1, d), lambda b, i: (0, 0))],
        out_specs=pl.BlockSpec((None, tm, d), lambda b, i: (b, i, 0)),
        out_shape=jax.ShapeDtypeStruct((bsz, l, d), F32),
        compiler_params=_cparams(("parallel", "parallel")),
        name="final_norm",
    )(xs, w.reshape(1, d))


def _rope_swap(w):
    j = np.arange(MLA_ROPE)
    first = (j % 16) < 8
    src = np.where(first, j + 8, j - 8)
    sign = np.where(first, -1.0, 1.0).astype(np.float32)
    return w[..., src] * sign


def _rope_tables(lc, l):
    n_rows = l // GRID_W
    row = jnp.repeat(jnp.arange(n_rows), GRID_W).astype(F32)
    col = jnp.tile(jnp.arange(GRID_W), n_rows).astype(F32)
    half = MLA_ROPE // 2
    inv = 1.0 / (ROPE_THETA ** (jnp.arange(0, half, 2, dtype=F32) / half))
    ar = row[:, None] * inv
    ac = col[:, None] * inv
    cos_l = jnp.concatenate([jnp.ones((l, MLA_NOPE), F32), jnp.cos(ar), jnp.cos(ar), jnp.cos(ac), jnp.cos(ac),
                             jnp.ones((l, HEAD_PAD - MLA_NOPE - MLA_ROPE), F32)], axis=1)
    sin_l = jnp.concatenate([jnp.zeros((l, MLA_NOPE), F32), jnp.sin(ar), jnp.sin(ar), jnp.sin(ac), jnp.sin(ac),
                             jnp.zeros((l, HEAD_PAD - MLA_NOPE - MLA_ROPE), F32)], axis=1)
    cos_t = jnp.concatenate([jnp.ones((lc, HEAD_PAD), F32), cos_l], axis=0)
    sin_t = jnp.concatenate([jnp.zeros((lc, HEAD_PAD), F32), sin_l], axis=0)
    return cos_t, sin_t


def _even_weights(w_in, w_uq, w_ukv, w_out):
    d = w_in.shape[0]
    o = np.cumsum([0, Q_LORA, KV_LORA, MLA_ROPE, SSM_DINNER, SSM_XBC, 2 * SSM_HEADS])
    cq, ckv, kr, z, xbc, dt = (w_in[:, o[i]:o[i + 1]] for i in range(6))
    zc = lambda n: jnp.zeros((d, n), F32)
    w_in_p = jnp.concatenate(
        [cq, ckv, z, xbc,
         zc(MLA_NOPE), kr, zc(HEAD_PAD - MLA_NOPE - MLA_ROPE),
         zc(MLA_NOPE), _rope_swap(kr), zc(HEAD_PAD - MLA_NOPE - MLA_ROPE),
         dt, zc(LANES - 2 * SSM_HEADS)], axis=1).astype(BF16)
    assert w_in_p.shape[1] == EV_N
    uq = w_uq.reshape(Q_LORA, MLA_HEADS, MLA_NOPE + MLA_ROPE)
    q_nope, q_rope = uq[..., :MLA_NOPE], uq[..., MLA_NOPE:]
    zq = lambda n: jnp.zeros((Q_LORA, MLA_HEADS, n), F32)
    npad = HEAD_PAD - MLA_NOPE - MLA_ROPE
    wq = jnp.concatenate([q_nope, q_rope, zq(npad)], axis=-1).reshape(Q_LORA, -1).astype(BF16)
    wqs = jnp.concatenate([zq(MLA_NOPE), _rope_swap(q_rope), zq(npad)], axis=-1).reshape(Q_LORA, -1).astype(BF16)
    ukv = w_ukv.reshape(KV_LORA, MLA_HEADS, MLA_NOPE + MLA_V)
    k_nope, vw = ukv[..., :MLA_NOPE], ukv[..., MLA_NOPE:]
    wk = jnp.concatenate([k_nope, jnp.zeros((KV_LORA, MLA_HEADS, HEAD_PAD - MLA_NOPE), F32)], axis=-1)
    wk = wk.reshape(KV_LORA, -1).astype(BF16)
    vpair = vw.reshape(KV_LORA, MLA_HEADS // 2, 2, MLA_V)
    zv = jnp.zeros((KV_LORA, MLA_HEADS // 2, MLA_V), F32)
    wva = jnp.concatenate([vpair[:, :, 0], zv], axis=-1).reshape(KV_LORA, -1).astype(BF16)
    wvb = jnp.concatenate([zv, vpair[:, :, 1]], axis=-1).reshape(KV_LORA, -1).astype(BF16)
    hv = MLA_HEADS * MLA_V
    return w_in_p, wq, wqs, wk, wva, wvb, w_out[:hv].astype(BF16), w_out[hv:].astype(BF16)


def _odd_weights(w_in):
    d = w_in.shape[0]
    ngate = 4 * MLSTM_HEADS
    w = jnp.concatenate([w_in, jnp.zeros((d, OD_N - OD_G - ngate), F32)], axis=1).astype(BF16)
    assert w.shape[1] == OD_N
    return w


def kernel(x, c, ctx, c_ctx, ada_w, ada_b, norm_mix_w, norm_mlp_w, mlp_w1, mlp_w2, ev_w_in, ev_q_norm_w, ev_w_uq, ev_kv_norm_w, ev_w_ukv, ev_conv_w, ev_conv_b, ev_dt_bias, ev_a_log, ev_d_skip, ev_ssm_norm_w, ev_w_out, od_w_in, od_conv_w, od_conv_b, od_i_bias, od_f_bias, od_head_norm_w, od_w_out, final_norm_w):
    bsz, l, d = x.shape
    lc = ctx.shape[1]
    depth = ada_w.shape[0]
    tm = 256 if lc % 256 == 0 else CHUNK
    assert lc % tm == 0 and l % tm == 0 and l % GRID_W == 0
    nct = lc // tm

    rows = -(-(bsz + 1) // SUBLANES) * SUBLANES
    svec = jnp.concatenate([c, c_ctx[None], jnp.zeros((rows - bsz - 1, d), F32)], axis=0)
    mods = _ada(svec, ada_w, ada_b)
    mod_ctx = jnp.broadcast_to(mods[:, bsz][:, None], (depth, bsz, 6 * d))
    mod_all = jnp.stack([mod_ctx, mods[:, :bsz]], axis=2)[:, :, :, None, :]

    cos_t, sin_t = _rope_tables(lc, l)
    xs = jnp.concatenate([ctx, x], axis=1)

    for layer in range(depth):
        mod = mod_all[layer]
        if layer % 2 == 0:
            e = layer // 2
            w_in_p, wq, wqs, wk, wva, wvb, wo_a, wo_s = _even_weights(ev_w_in[e], ev_w_uq[e], ev_w_ukv[e], ev_w_out[e])
            (p,) = _inproj(xs, norm_mix_w[layer], mod, w_in_p, tm, nct)
            q, k, va, vb = _mla_prep(p, cos_t, sin_t, ev_q_norm_w[e], ev_kv_norm_w[e], wq, wqs, wk, wva, wvb, tm)
            o_attn = _attention(q, k, va, vb, tm, lc)
            cv = _conv(p, EV_XBC, ev_conv_w[e], ev_conv_b[e], lc, l)
            s_ssd = _ssd(cv, p, ev_dt_bias[e], ev_a_log[e], ev_d_skip[e], ev_ssm_norm_w[e], lc)
            mix, mix_w = [o_attn, s_ssd], [wo_a, wo_s]
        else:
            o = layer // 2
            p, v_t = _inproj(xs, norm_mix_w[layer], mod, _odd_weights(od_w_in[o]), tm, nct,
                             tcols=(OD_V, MLSTM_VW))
            cv = _conv(p, OD_QK, od_conv_w[o], od_conv_b[o], lc, l)
            q_t = jnp.swapaxes(cv[:, :, :MLSTM_QK], 1, 2)
            m = _mlstm(cv, q_t, v_t, p, od_i_bias[o], od_f_bias[o], od_head_norm_w[o], lc)
            mix, mix_w = [m], [od_w_out[o].astype(BF16)]
        xs = _mix_mlp(mix, mix_w, xs, norm_mlp_w[layer], mod, mlp_w1[layer].astype(BF16),
                      mlp_w2[layer].astype(BF16), tm, nct)
    return _final_norm(xs, final_norm_w, tm, nct, l)
```

```python
import functools

import jax
import jax.numpy as jnp
import numpy as np
from jax import lax
from jax.experimental import pallas as pl
from jax.experimental.pallas import tpu as pltpu

F32 = jnp.float32
BF16 = jnp.bfloat16

EPS = 1e-6
GRID_W = 64
CHUNK = 128
CONV_K = 5
LANES = 128
SUBLANES = 8

MLA_HEADS = 8
MLA_NOPE = 64
MLA_ROPE = 32
MLA_V = 64
Q_LORA = 256
KV_LORA = 256
ROPE_THETA = 10000.0
MLA_SCALE = (MLA_NOPE + MLA_ROPE) ** -0.5
HEAD_PAD = 128
LOG2E = 1.4426950408889634
ATT_SUM_LANE_A = MLA_V
ATT_SUM_LANE_B = 0

SSM_HEADS = 8
SSM_HEADDIM = 64
SSM_DINNER = SSM_HEADS * SSM_HEADDIM
SSM_GROUPS = 2
SSM_STATE = 128
SSM_XBC = SSM_DINNER + 2 * SSM_GROUPS * SSM_STATE

MLSTM_HEADS = 8
MLSTM_DQK = 64
MLSTM_DV = 128
MLSTM_QK = MLSTM_HEADS * MLSTM_DQK
MLSTM_VW = MLSTM_HEADS * MLSTM_DV

VMEM_LIMIT = 56 * 1024 * 1024

EV_CQ, EV_CKV, EV_Z, EV_XBC = 0, 256, 512, 1024
EV_ROPE_A, EV_ROPE_B, EV_DT, EV_N = 2048, 2176, 2304, 2432
OD_QK, OD_V, OD_O, OD_G, OD_N = 0, 1024, 2048, 3072, 3200


def _cparams(sem):
    return pltpu.CompilerParams(dimension_semantics=sem, vmem_limit_bytes=VMEM_LIMIT)


def _dot(a, b):
    return jnp.dot(a, b, preferred_element_type=F32)


def _dot_nt(a, b):
    return lax.dot_general(a, b, (((1,), (1,)), ((), ())), preferred_element_type=F32)


def _dot_tn(a, b):
    return lax.dot_general(a, b, (((0,), (0,)), ((), ())), preferred_element_type=F32)


def _sigmoid(x):
    return 1.0 / (1.0 + jnp.exp(-x))


def _softplus(x):
    return jnp.maximum(x, 0.0) + jnp.log(1.0 + jnp.exp(-jnp.abs(x)))


def _rms(x, w):
    ms = jnp.mean(x * x, axis=-1, keepdims=True)
    return x * lax.rsqrt(ms + EPS) * w


def _dot_f32_rows(x, m_bf16):
    hi = x.astype(BF16)
    r1 = x - hi.astype(F32)
    mid = r1.astype(BF16)
    lo = (r1 - mid.astype(F32)).astype(BF16)
    return _dot(hi, m_bf16) + _dot(mid, m_bf16) + _dot(lo, m_bf16)


def _ada_kernel(s_ref, w_ref, b_ref, o_ref):
    s = s_ref[...]
    s = (s * _sigmoid(s)).astype(BF16)
    o_ref[...] = _dot(s, w_ref[...].astype(BF16)) + b_ref[...]


def _ada(svec, ada_w, ada_b):
    depth, d, n6 = ada_w.shape
    rows = svec.shape[0]
    tn = 1024
    return pl.pallas_call(
        _ada_kernel,
        grid=(depth, n6 // tn),
        in_specs=[pl.BlockSpec((rows, d), lambda l, j: (0, 0)),
                  pl.BlockSpec((None, d, tn), lambda l, j: (l, 0, j)),
                  pl.BlockSpec((None, 1, tn), lambda l, j: (l, 0, j))],
        out_specs=pl.BlockSpec((None, rows, tn), lambda l, j: (l, 0, j)),
        out_shape=jax.ShapeDtypeStruct((depth, rows, n6), F32),
        compiler_params=_cparams(("parallel", "parallel")),
        name="ada",
    )(svec, ada_w, ada_b.reshape(depth, 1, n6))


def _inproj_kernel(x_ref, nw_ref, sh_ref, sc_ref, w_ref, o_ref, *t_refs, tcols):
    h = _rms(x_ref[...], nw_ref[...]) * (1.0 + sc_ref[...]) + sh_ref[...]
    res = _dot(h.astype(BF16), w_ref[...])
    o_ref[...] = res
    if tcols is not None:
        t_refs[0][...] = jnp.transpose(res[:, tcols[0]:tcols[0] + tcols[1]])


def _mod_spec(d, chunk, nct):
    return pl.BlockSpec((None, None, 1, d), lambda b, i: (b, jnp.where(i >= nct, 1, 0), 0, chunk))


def _inproj(xs, nw, mod, w, tm, nct, tcols=None):
    bsz, t, d = xs.shape
    n = w.shape[1]
    out_specs = [pl.BlockSpec((None, tm, n), lambda b, i: (b, i, 0))]
    out_shape = [jax.ShapeDtypeStruct((bsz, t, n), F32)]
    if tcols is not None:
        out_specs.append(pl.BlockSpec((None, tcols[1], tm), lambda b, i: (b, 0, i)))
        out_shape.append(jax.ShapeDtypeStruct((bsz, tcols[1], t), F32))
    return pl.pallas_call(
        functools.partial(_inproj_kernel, tcols=tcols),
        grid=(bsz, t // tm),
        in_specs=[pl.BlockSpec((None, tm, d), lambda b, i: (b, i, 0)),
                  pl.BlockSpec((1, d), lambda b, i: (0, 0)),
                  _mod_spec(d, 0, nct), _mod_spec(d, 1, nct),
                  pl.BlockSpec((d, n), lambda b, i: (0, 0))],
        out_specs=out_specs,
        out_shape=out_shape,
        compiler_params=_cparams(("parallel", "parallel")),
        name="inproj",
    )(xs, nw.reshape(1, d), mod, mod, w)


def _conv_kernel(x_ref, w_ref, b_ref, o_ref, pad_ref, *, lc, l):
    ct = x_ref.shape[1]
    zeros = jnp.zeros((SUBLANES, ct), F32)
    pad_ref[0:SUBLANES, :] = zeros
    pad_ref[SUBLANES:SUBLANES + lc, :] = x_ref[0:lc, :]
    pad_ref[SUBLANES + lc:2 * SUBLANES + lc, :] = zeros
    pad_ref[2 * SUBLANES + lc:2 * SUBLANES + lc + l, :] = x_ref[lc:lc + l, :]
    pad_ref[2 * SUBLANES + lc + l:3 * SUBLANES + lc + l, :] = zeros
    w = w_ref[...]
    bias = b_ref[...]
    for r0 in range(0, lc + l, CHUNK):
        base = (SUBLANES if r0 < lc else 2 * SUBLANES) + r0
        acc = bias + w[0:1, :] * pad_ref[base - 2:base - 2 + CHUNK, :]
        for k in range(1, CONV_K):
            acc = acc + w[k:k + 1, :] * pad_ref[base + k - 2:base + k - 2 + CHUNK, :]
        o_ref[r0:r0 + CHUNK, :] = acc * _sigmoid(acc)


def _conv(p, col0, conv_w, conv_b, lc, l):
    bsz, t, _ = p.shape
    c = conv_w.shape[1]
    ct = 256
    cb0 = col0 // ct
    return pl.pallas_call(
        functools.partial(_conv_kernel, lc=lc, l=l),
        grid=(bsz, c // ct),
        in_specs=[pl.BlockSpec((None, t, ct), lambda b, j: (b, 0, cb0 + j)),
                  pl.BlockSpec((CONV_K, ct), lambda b, j: (0, j)),
                  pl.BlockSpec((1, ct), lambda b, j: (0, j))],
        out_specs=pl.BlockSpec((None, t, ct), lambda b, j: (b, 0, j)),
        out_shape=jax.ShapeDtypeStruct((bsz, t, c), F32),
        scratch_shapes=[pltpu.VMEM((t + 3 * SUBLANES, ct), F32)],
        compiler_params=_cparams(("parallel", "parallel")),
        name="dwconv_silu",
    )(p, conv_w, conv_b.reshape(1, c))


def _mla_prep_kernel(cq_ref, ckv_ref, ra_ref, rb_ref, cos_ref, sin_ref, qnw_ref, kvnw_ref,
                     wq_ref, wqs_ref, wk_ref, wva_ref, wvb_ref, q_ref, k_ref, va_ref, vb_ref):
    cos = cos_ref[...]
    sin = sin_ref[...]
    cos8 = jnp.concatenate([cos] * MLA_HEADS, axis=1)
    sin8 = jnp.concatenate([sin] * MLA_HEADS, axis=1)
    cqn = _rms(cq_ref[...], qnw_ref[...]).astype(BF16)
    q = _dot(cqn, wq_ref[...]) * cos8 + _dot(cqn, wqs_ref[...]) * sin8
    q_ref[...] = q.astype(BF16)
    ckvn = _rms(ckv_ref[...], kvnw_ref[...]).astype(BF16)
    kr = ra_ref[...] * cos + rb_ref[...] * sin
    k = _dot(ckvn, wk_ref[...]) + jnp.concatenate([kr] * MLA_HEADS, axis=1)
    k_ref[...] = k.astype(BF16)
    lane = lax.broadcasted_iota(jnp.int32, (1, va_ref.shape[1]), 1) % (2 * MLA_V)
    va_ref[...] = (_dot(ckvn, wva_ref[...]) + (lane == ATT_SUM_LANE_A).astype(F32)).astype(BF16)
    vb_ref[...] = (_dot(ckvn, wvb_ref[...]) + (lane == ATT_SUM_LANE_B).astype(F32)).astype(BF16)


def _mla_prep(p, cos_t, sin_t, qnw, kvnw, wq, wqs, wk, wva, wvb, tm):
    bsz, t, _ = p.shape
    hq = MLA_HEADS * HEAD_PAD
    hv = MLA_HEADS * MLA_V
    full = lambda shape: pl.BlockSpec(shape, lambda b, i: (0, 0))
    return pl.pallas_call(
        _mla_prep_kernel,
        grid=(bsz, t // tm),
        in_specs=[pl.BlockSpec((None, tm, Q_LORA), lambda b, i: (b, i, EV_CQ // Q_LORA)),
                  pl.BlockSpec((None, tm, KV_LORA), lambda b, i: (b, i, EV_CKV // KV_LORA)),
                  pl.BlockSpec((None, tm, LANES), lambda b, i: (b, i, EV_ROPE_A // LANES)),
                  pl.BlockSpec((None, tm, LANES), lambda b, i: (b, i, EV_ROPE_B // LANES)),
                  pl.BlockSpec((tm, LANES), lambda b, i: (i, 0)),
                  pl.BlockSpec((tm, LANES), lambda b, i: (i, 0)),
                  full((1, Q_LORA)), full((1, KV_LORA)),
                  full((Q_LORA, hq)), full((Q_LORA, hq)), full((KV_LORA, hq)),
                  full((KV_LORA, hv)), full((KV_LORA, hv))],
        out_specs=[pl.BlockSpec((None, tm, hq), lambda b, i: (b, i, 0)),
                   pl.BlockSpec((None, tm, hq), lambda b, i: (b, i, 0)),
                   pl.BlockSpec((None, tm, hv), lambda b, i: (b, i, 0)),
                   pl.BlockSpec((None, tm, hv), lambda b, i: (b, i, 0))],
        out_shape=[jax.ShapeDtypeStruct((bsz, t, hq), BF16), jax.ShapeDtypeStruct((bsz, t, hq), BF16),
                   jax.ShapeDtypeStruct((bsz, t, hv), BF16), jax.ShapeDtypeStruct((bsz, t, hv), BF16)],
        compiler_params=_cparams(("parallel", "parallel")),
        name="mla_prep",
    )(p, p, p, p, cos_t, sin_t, qnw.reshape(1, -1), kvnw.reshape(1, -1), wq, wqs, wk, wva, wvb)


def _attn_kernel(q_ref, k_ref, va_ref, vb_ref, o_ref, *, lc, nct):
    i = pl.program_id(2)

    def attend(nk):
        lane = lax.broadcasted_iota(jnp.int32, (1, 2 * MLA_V), 1)
        acc = None
        scores = [_dot_nt(q_ref[:, e * HEAD_PAD:(e + 1) * HEAD_PAD], k_ref[0:nk, e * HEAD_PAD:(e + 1) * HEAD_PAD])
                  for e in range(2)]
        for e, (v_ref, sum_lane) in enumerate(((va_ref, ATT_SUM_LANE_A), (vb_ref, ATT_SUM_LANE_B))):
            s = scores[e]
            m = jnp.max(s, axis=-1, keepdims=True)
            p = jnp.exp2((s - m) * (MLA_SCALE * LOG2E))
            o = _dot(p.astype(BF16), v_ref[0:nk, :])
            l = jnp.sum(jnp.where(lane == sum_lane, o, 0.0), axis=-1, keepdims=True)
            o = jnp.where((lane // MLA_V) == e, o / l, 0.0)
            acc = o if acc is None else acc + o
        o_ref[...] = acc.astype(BF16)

    @pl.when(i < nct)
    def _():
        attend(lc)

    @pl.when(i >= nct)
    def _():
        attend(k_ref.shape[0])


def _attention(q, k, va, vb, tq, lc):
    bsz, t, _ = q.shape
    npair = MLA_HEADS // 2
    return pl.pallas_call(
        functools.partial(_attn_kernel, lc=lc, nct=lc // tq),
        grid=(bsz, npair, t // tq),
        in_specs=[pl.BlockSpec((None, tq, 2 * HEAD_PAD), lambda b, p, i: (b, i, p)),
                  pl.BlockSpec((None, t, 2 * HEAD_PAD), lambda b, p, i: (b, 0, p)),
                  pl.BlockSpec((None, t, 2 * MLA_V), lambda b, p, i: (b, 0, p)),
                  pl.BlockSpec((None, t, 2 * MLA_V), lambda b, p, i: (b, 0, p))],
        out_specs=pl.BlockSpec((None, tq, 2 * MLA_V), lambda b, p, i: (b, i, p)),
        out_shape=jax.ShapeDtypeStruct((bsz, t, MLA_HEADS * MLA_V), BF16),
        compiler_params=_cparams(("parallel", "parallel", "parallel")),
        name="mla_attention",
    )(q, k, va, vb)


def _chunk_of_step(s, n, ncc):
    r = s - n
    rev = jnp.where(r < ncc, ncc - 1 - r, n - 1 - (r - ncc))
    return jnp.where(s < n, s, rev)


def _scan_masks(rev):
    ti = lax.broadcasted_iota(jnp.int32, (CHUNK, CHUNK), 0)
    si = lax.broadcasted_iota(jnp.int32, (CHUNK, CHUNK), 1)
    eye = ti == si
    if rev:
        cum_m = (ti >= si).astype(BF16)
        mask = si >= ti
        last = si[0:1, :] == 0
    else:
        cum_m = (ti <= si).astype(BF16)
        mask = si <= ti
        last = si[0:1, :] == CHUNK - 1
    return cum_m, mask, eye, last


def _to_col(row, eye):
    return jnp.sum(jnp.where(eye, row, 0.0), axis=1, keepdims=True)


def _pick_last(row, last):
    return jnp.sum(jnp.where(last, row, 0.0), axis=1, keepdims=True)


def _ssd_kernel(xs_ref, bm_ref, cm_ref, z_ref, dt_ref, dtb_ref, alog_ref, dskip_ref, nw_ref,
                o_ref, ybuf, ht, *, n, ncc):
    s = pl.program_id(1)
    c = _chunk_of_step(s, n, ncc)
    hpg = SSM_HEADS // SSM_GROUPS
    gw = hpg * SSM_HEADDIM

    @pl.when(jnp.logical_or(s == 0, s == n))
    def _():
        ht[...] = jnp.zeros_like(ht)

    def chunk(rev):
        d = 1 if rev else 0
        cum_m, mask, eye, last = _scan_masks(rev)
        rows = slice(d * SSM_HEADS, (d + 1) * SSM_HEADS)
        dt_rows = jnp.transpose(dt_ref[...])
        dt = _softplus(dt_rows[rows, :] + dtb_ref[rows, :])
        da = dt * (-jnp.exp(alog_ref[rows, :]))
        cum = _dot_f32_rows(da, cum_m)
        head_of_lane = lax.broadcasted_iota(jnp.int32, (CHUNK, gw), 1) // SSM_HEADDIM
        xs = xs_ref[...]
        ys = []
        for g in range(SSM_GROUPS):
            bmb = bm_ref[:, g * SSM_STATE:(g + 1) * SSM_STATE].astype(BF16)
            cmb = cm_ref[:, g * SSM_STATE:(g + 1) * SSM_STATE].astype(BF16)
            cb = _dot_nt(cmb, bmb)
            xg = xs[:, g * gw:(g + 1) * gw]
            xgb = xg.astype(BF16)
            yg = jnp.zeros((CHUNK, gw), F32)
            eg = jnp.zeros((CHUNK, gw), F32)
            teg = jnp.zeros((CHUNK, gw), F32)
            dg = jnp.zeros((1, gw), F32)
            for r in range(hpg):
                h = g * hpg + r
                row = cum[h:h + 1, :]
                dtrow = dt[h:h + 1, :]
                col = _to_col(row, eye)
                dtcol = _to_col(dtrow, eye)
                tot = _pick_last(row, last)
                decay = jnp.exp(jnp.where(mask, col - row, -jnp.inf))
                w = cb * decay * dtrow
                yh = _dot(w.astype(BF16), xgb)
                sel = head_of_lane == r
                yg = jnp.where(sel, yh, yg)
                eg = jnp.where(sel, jnp.exp(col), eg)
                teg = jnp.where(sel, jnp.exp(tot - col) * dtcol, teg)
                dg = jnp.where(sel[0:1, :], jnp.exp(tot), dg)
            htg = ht[g]
            yg = yg + _dot(cmb, htg.astype(BF16)) * eg
            ht[g] = htg * dg + _dot_tn(bmb, (xg * teg).astype(BF16))
            ys.append(yg)
        return jnp.concatenate(ys, axis=1), xs

    @pl.when(s < n)
    def _():
        y, xs = chunk(False)
        ybuf[c] = y + dskip_ref[...] * xs

    @pl.when(s >= n)
    def _():
        y, _ = chunk(True)
        y = y + ybuf[c]
        z = z_ref[...]
        gated = y * (z * _sigmoid(z))
        outs = []
        for g in range(SSM_GROUPS):
            gg = gated[:, g * gw:(g + 1) * gw]
            outs.append(gg * lax.rsqrt(jnp.mean(gg * gg, axis=-1, keepdims=True) + EPS))
        o_ref[...] = (jnp.concatenate(outs, axis=1) * nw_ref[...]).astype(BF16)


def _ssd(cv, p, dt_bias, a_log, d_skip, norm_w, lc):
    bsz, t, _ = cv.shape
    n = t // CHUNK
    ncc = lc // CHUNK
    cidx = lambda s: _chunk_of_step(s, n, ncc)
    cidx_out = lambda s: _chunk_of_step(jnp.maximum(s, n), n, ncc)
    nst = SSM_GROUPS * SSM_STATE
    small = lambda shape: pl.BlockSpec(shape, lambda b, s: (0, 0))
    return pl.pallas_call(
        functools.partial(_ssd_kernel, n=n, ncc=ncc),
        grid=(bsz, 2 * n),
        in_specs=[pl.BlockSpec((None, CHUNK, SSM_DINNER), lambda b, s: (b, cidx(s), 0)),
                  pl.BlockSpec((None, CHUNK, nst), lambda b, s: (b, cidx(s), SSM_DINNER // nst)),
                  pl.BlockSpec((None, CHUNK, nst), lambda b, s: (b, cidx(s), SSM_DINNER // nst + 1)),
                  pl.BlockSpec((None, CHUNK, SSM_DINNER), lambda b, s: (b, cidx_out(s), EV_Z // SSM_DINNER)),
                  pl.BlockSpec((None, CHUNK, LANES), lambda b, s: (b, cidx(s), EV_DT // LANES)),
                  small((2 * SSM_HEADS, 1)), small((2 * SSM_HEADS, 1)),
                  small((1, SSM_DINNER)), small((1, SSM_DINNER))],
        out_specs=pl.BlockSpec((None, CHUNK, SSM_DINNER), lambda b, s: (b, cidx_out(s), 0)),
        out_shape=jax.ShapeDtypeStruct((bsz, t, SSM_DINNER), BF16),
        scratch_shapes=[pltpu.VMEM((n, CHUNK, SSM_DINNER), F32),
                        pltpu.VMEM((SSM_GROUPS, SSM_STATE, SSM_DINNER // SSM_GROUPS), F32)],
        compiler_params=_cparams(("parallel", "arbitrary")),
        name="ssd_scan",
    )(cv, cv, cv, p, p, dt_bias.reshape(-1, 1), a_log.reshape(-1, 1),
      jnp.repeat(d_skip, SSM_HEADDIM).reshape(1, -1), norm_w.reshape(1, -1))


MLSTM_XR = 16


def _mlstm_kernel(k_ref, qt_ref, vt_ref, o_ref, g_ref, ib_ref, fb_ref, nw_ref,
                  out_ref, hbuf, cst, mst, *, n, ncc):
    s = pl.program_id(1)
    c = _chunk_of_step(s, n, ncc)
    nh, dk, dv = MLSTM_HEADS, MLSTM_DQK, MLSTM_DV
    pw = 2 * dk

    @pl.when(jnp.logical_or(s == 0, s == n))
    def _():
        cst[...] = jnp.zeros_like(cst)
        mst[...] = jnp.zeros_like(mst)

    def chunk(rev):
        d = 1 if rev else 0
        si = lax.broadcasted_iota(jnp.int32, (CHUNK, CHUNK), 0)
        ti = lax.broadcasted_iota(jnp.int32, (CHUNK, CHUNK), 1)
        seen = (si >= ti) if rev else (si <= ti)
        cum_m = seen.astype(BF16)
        g_rows = jnp.transpose(g_ref[...])
        ipre = g_rows[d * nh:(d + 1) * nh, :] + ib_ref[d * nh:(d + 1) * nh, :]
        fpre = g_rows[(2 + d) * nh:(3 + d) * nh, :] + fb_ref[d * nh:(d + 1) * nh, :]
        bcum = _dot_f32_rows(-_softplus(-fpre), cum_m)
        lane = lax.broadcasted_iota(jnp.int32, (1, CHUNK), 1)
        btot = jnp.sum(jnp.where(lane == (0 if rev else CHUNK - 1), bcum, 0.0), axis=1, keepdims=True)
        cs = ipre - bcum
        wend = btot + cs
        m_in = mst[:, 0:1]
        m_new = jnp.maximum(btot + m_in, jnp.max(wend, axis=1, keepdims=True))
        keep = jnp.exp(btot + m_in - m_new)
        wt = jnp.exp(wend - m_new)
        mst[...] = jnp.broadcast_to(m_new, mst.shape)

        row_half = lax.broadcasted_iota(jnp.int32, (pw, CHUNK), 0) // dk
        lane_half = lax.broadcasted_iota(jnp.int32, (CHUNK, pw), 1) // dk
        ones_rows = (lax.broadcasted_iota(jnp.int32, (MLSTM_XR, CHUNK), 0) == 0).astype(F32)
        zblk = jnp.zeros((CHUNK, CHUNK), BF16)
        outs = []
        sts, inters, vtps = [], [], []
        for p in range(nh // 2):
            kp = k_ref[:, p * pw:(p + 1) * pw] * (dk ** -0.5)
            qtp = qt_ref[p * pw:(p + 1) * pw, :]
            qbd = jnp.concatenate([jnp.where(row_half == 0, qtp, 0.0), jnp.where(row_half == 1, qtp, 0.0)],
                                  axis=1).astype(BF16)
            sts.append(_dot(kp.astype(BF16), qbd))
            cp = cst[p]
            inters.append(_dot(cp.astype(BF16), qbd))
            vtp = jnp.concatenate(
                [jnp.concatenate([vt_ref[(2 * p + e) * dv:(2 * p + e + 1) * dv, :], ones_rows], axis=0)
                 for e in range(2)], axis=1)
            vtps.append(vtp.astype(BF16))
            wtp = jnp.concatenate([wt[2 * p:2 * p + 1, :], wt[2 * p + 1:2 * p + 2, :]], axis=1)
            kbd = jnp.concatenate([jnp.where(lane_half == 0, kp, 0.0), jnp.where(lane_half == 1, kp, 0.0)],
                                  axis=0).astype(BF16)
            keepp = jnp.where(lane_half[0:1, :] == 0, keep[2 * p:2 * p + 1, :], keep[2 * p + 1:2 * p + 2, :])
            cst[p] = cp * keepp + _dot((vtp * wtp).astype(BF16), kbd)
        for p in range(nh // 2):
            st = sts[p]
            ats, keepqs, mrows = [], [], []
            for e in range(2):
                h = 2 * p + e
                cs_st = jnp.transpose(jnp.broadcast_to(cs[h:h + 1, :], (CHUNK, CHUNK)))
                ex = jnp.where(seen, cs_st, -jnp.inf)
                mrow = jnp.maximum(m_in[h:h + 1, :], jnp.max(ex, axis=0, keepdims=True))
                ats.append((st[:, e * CHUNK:(e + 1) * CHUNK] * jnp.exp(ex - mrow)).astype(BF16))
                keepqs.append(jnp.exp(m_in[h:h + 1, :] - mrow))
                mrows.append(mrow)
            abd = jnp.concatenate([jnp.concatenate([ats[0], zblk], axis=1),
                                   jnp.concatenate([zblk, ats[1]], axis=1)], axis=0)
            nd = _dot(vtps[p], abd) + inters[p] * jnp.concatenate(keepqs, axis=1)
            for e in range(2):
                h = 2 * p + e
                num = nd[0:dv, e * CHUNK:(e + 1) * CHUNK]
                den = nd[dv:dv + 1, e * CHUNK:(e + 1) * CHUNK]
                mt = bcum[h:h + 1, :] + mrows[e]
                outs.append(num * (1.0 / jnp.maximum(jnp.abs(den), jnp.exp(-mt))))
        return jnp.concatenate(outs, axis=0)

    @pl.when(s < n)
    def _():
        hbuf[c] = chunk(False)

    @pl.when(s >= n)
    def _():
        ht = chunk(True) + hbuf[c]
        outs = []
        for h in range(nh):
            hh = ht[h * dv:(h + 1) * dv, :]
            hn = hh * lax.rsqrt(jnp.mean(hh * hh, axis=0, keepdims=True) + EPS)
            outs.append(jnp.transpose(hn))
        hn = jnp.concatenate(outs, axis=1) * nw_ref[...]
        out_ref[...] = (_sigmoid(o_ref[...]) * hn).astype(BF16)


def _mlstm(cv, q_t, v_t, p, i_bias, f_bias, head_norm_w, lc):
    bsz, t, _ = cv.shape
    n = t // CHUNK
    ncc = lc // CHUNK
    cidx = lambda s: _chunk_of_step(s, n, ncc)
    cidx_out = lambda s: _chunk_of_step(jnp.maximum(s, n), n, ncc)
    nh = MLSTM_HEADS
    small = lambda shape: pl.BlockSpec(shape, lambda b, s: (0, 0))
    return pl.pallas_call(
        functools.partial(_mlstm_kernel, n=n, ncc=ncc),
        grid=(bsz, 2 * n),
        in_specs=[pl.BlockSpec((None, CHUNK, MLSTM_QK), lambda b, s: (b, cidx(s), 1)),
                  pl.BlockSpec((None, MLSTM_QK, CHUNK), lambda b, s: (b, 0, cidx(s))),
                  pl.BlockSpec((None, MLSTM_VW, CHUNK), lambda b, s: (b, 0, cidx(s))),
                  pl.BlockSpec((None, CHUNK, MLSTM_VW), lambda b, s: (b, cidx_out(s), OD_O // MLSTM_VW)),
                  pl.BlockSpec((None, CHUNK, LANES), lambda b, s: (b, cidx(s), OD_G // LANES)),
                  small((2 * nh, 1)), small((2 * nh, 1)), small((1, MLSTM_VW))],
        out_specs=pl.BlockSpec((None, CHUNK, MLSTM_VW), lambda b, s: (b, cidx_out(s), 0)),
        out_shape=jax.ShapeDtypeStruct((bsz, t, MLSTM_VW), BF16),
        scratch_shapes=[pltpu.VMEM((n, MLSTM_VW, CHUNK), F32),
                        pltpu.VMEM((nh // 2, MLSTM_DV + MLSTM_XR, 2 * MLSTM_DQK), F32),
                        pltpu.VMEM((nh, LANES), F32)],
        compiler_params=_cparams(("parallel", "arbitrary")),
        name="mlstm_scan",
    )(cv, q_t, v_t, p, p, i_bias.reshape(-1, 1), f_bias.reshape(-1, 1), head_norm_w.reshape(1, -1))


def _mix_mlp_kernel(*refs, nin, tf):
    ms = refs[:nin]
    ws = refs[nin:2 * nin]
    x_ref, g1_ref, nw_ref, sh_ref, sc_ref, g2_ref, w1_ref, w2_ref, fw_ref, o_ref = refs[2 * nin:]
    mix = _dot(ms[0][...], ws[0][...])
    for m_ref, w_ref in zip(ms[1:], ws[1:]):
        mix = mix + _dot(m_ref[...], w_ref[...])
    x = x_ref[...] + g1_ref[...] * mix
    h = (_rms(x, nw_ref[...]) * (1.0 + sc_ref[...]) + sh_ref[...]).astype(BF16)
    acc = jnp.zeros(x.shape, F32)
    for f in range(0, w1_ref.shape[1], tf):
        u = jnp.maximum(_dot(h, w1_ref[:, f:f + tf]), 0.0)
        acc = acc + _dot((u * u).astype(BF16), w2_ref[f:f + tf, :])
    x = x + g2_ref[...] * acc
    o_ref[...] = x if fw_ref is None else _rms(x, fw_ref[...])


def _mix_mlp(ms, ws, xs, nw, mod, w1, w2, tm, nct, final_w=None):
    bsz, t, d = xs.shape
    dff = w1.shape[1]
    nin = len(ms)
    skip = nct if final_w is not None else 0
    rows = lambda width: pl.BlockSpec((None, tm, width), lambda b, i: (b, i + skip, 0))
    full = lambda shape: pl.BlockSpec(shape, lambda b, i: (0, 0))
    mspec = lambda chunk: _mod_spec(d, chunk, nct - skip)
    args = [*ms, *ws, xs, mod, nw.reshape(1, d), mod, mod, mod, w1, w2]
    in_specs = ([rows(m.shape[2]) for m in ms] + [full(w.shape) for w in ws]
                + [rows(d), mspec(2), full((1, d)), mspec(3), mspec(4), mspec(5), full((d, dff)), full((dff, d))])
    kern = functools.partial(_mix_mlp_kernel, nin=nin, tf=1024)
    if final_w is not None:
        args.append(final_w.reshape(1, d))
        in_specs.append(full((1, d)))
    else:
        kern = functools.partial(_no_final, kern, len(args))
    return pl.pallas_call(
        kern,
        grid=(bsz, t // tm - skip),
        in_specs=in_specs,
        out_specs=pl.BlockSpec((None, tm, d), lambda b, i: (b, i, 0)),
        out_shape=jax.ShapeDtypeStruct((bsz, t - skip * tm, d), F32),
        compiler_params=_cparams(("parallel", "parallel")),
        name="outproj_mlp_residual",
    )(*args)


def _no_final(kern, nargs, *refs):
    kern(*refs[:nargs], None, *refs[nargs:])


def _rope_swap(w):
    j = np.arange(MLA_ROPE)
    first = (j % 16) < 8
    src = np.where(first, j + 8, j - 8)
    sign = np.where(first, -1.0, 1.0).astype(np.float32)
    return w[..., src] * sign


def _rope_tables(lc, l):
    n_rows = l // GRID_W
    row = jnp.repeat(jnp.arange(n_rows), GRID_W).astype(F32)
    col = jnp.tile(jnp.arange(GRID_W), n_rows).astype(F32)
    half = MLA_ROPE // 2
    inv = 1.0 / (ROPE_THETA ** (jnp.arange(0, half, 2, dtype=F32) / half))
    ar = row[:, None] * inv
    ac = col[:, None] * inv
    cos_l = jnp.concatenate([jnp.ones((l, MLA_NOPE), F32), jnp.cos(ar), jnp.cos(ar), jnp.cos(ac), jnp.cos(ac),
                             jnp.ones((l, HEAD_PAD - MLA_NOPE - MLA_ROPE), F32)], axis=1)
    sin_l = jnp.concatenate([jnp.zeros((l, MLA_NOPE), F32), jnp.sin(ar), jnp.sin(ar), jnp.sin(ac), jnp.sin(ac),
                             jnp.zeros((l, HEAD_PAD - MLA_NOPE - MLA_ROPE), F32)], axis=1)
    cos_t = jnp.concatenate([jnp.ones((lc, HEAD_PAD), F32), cos_l], axis=0)
    sin_t = jnp.concatenate([jnp.zeros((lc, HEAD_PAD), F32), sin_l], axis=0)
    return cos_t, sin_t


def _even_weights(w_in, w_uq, w_ukv, w_out):
    d = w_in.shape[0]
    o = np.cumsum([0, Q_LORA, KV_LORA, MLA_ROPE, SSM_DINNER, SSM_XBC, 2 * SSM_HEADS])
    cq, ckv, kr, z, xbc, dt = (w_in[:, o[i]:o[i + 1]] for i in range(6))
    zc = lambda n: jnp.zeros((d, n), F32)
    w_in_p = jnp.concatenate(
        [cq, ckv, z, xbc,
         zc(MLA_NOPE), kr, zc(HEAD_PAD - MLA_NOPE - MLA_ROPE),
         zc(MLA_NOPE), _rope_swap(kr), zc(HEAD_PAD - MLA_NOPE - MLA_ROPE),
         dt, zc(LANES - 2 * SSM_HEADS)], axis=1).astype(BF16)
    assert w_in_p.shape[1] == EV_N
    uq = w_uq.reshape(Q_LORA, MLA_HEADS, MLA_NOPE + MLA_ROPE)
    q_nope, q_rope = uq[..., :MLA_NOPE], uq[..., MLA_NOPE:]
    zq = lambda n: jnp.zeros((Q_LORA, MLA_HEADS, n), F32)
    npad = HEAD_PAD - MLA_NOPE - MLA_ROPE
    wq = jnp.concatenate([q_nope, q_rope, zq(npad)], axis=-1).reshape(Q_LORA, -1).astype(BF16)
    wqs = jnp.concatenate([zq(MLA_NOPE), _rope_swap(q_rope), zq(npad)], axis=-1).reshape(Q_LORA, -1).astype(BF16)
    ukv = w_ukv.reshape(KV_LORA, MLA_HEADS, MLA_NOPE + MLA_V)
    k_nope, vw = ukv[..., :MLA_NOPE], ukv[..., MLA_NOPE:]
    wk = jnp.concatenate([k_nope, jnp.zeros((KV_LORA, MLA_HEADS, HEAD_PAD - MLA_NOPE), F32)], axis=-1)
    wk = wk.reshape(KV_LORA, -1).astype(BF16)
    vpair = vw.reshape(KV_LORA, MLA_HEADS // 2, 2, MLA_V)
    zv = jnp.zeros((KV_LORA, MLA_HEADS // 2, MLA_V), F32)
    wva = jnp.concatenate([vpair[:, :, 0], zv], axis=-1).reshape(KV_LORA, -1).astype(BF16)
    wvb = jnp.concatenate([zv, vpair[:, :, 1]], axis=-1).reshape(KV_LORA, -1).astype(BF16)
    hv = MLA_HEADS * MLA_V
    return w_in_p, wq, wqs, wk, wva, wvb, w_out[:hv].astype(BF16), w_out[hv:].astype(BF16)


def _odd_weights(w_in):
    d = w_in.shape[0]
    ngate = 4 * MLSTM_HEADS
    w = jnp.concatenate([w_in, jnp.zeros((d, OD_N - OD_G - ngate), F32)], axis=1).astype(BF16)
    assert w.shape[1] == OD_N
    return w


def kernel(x, c, ctx, c_ctx, ada_w, ada_b, norm_mix_w, norm_mlp_w, mlp_w1, mlp_w2, ev_w_in, ev_q_norm_w, ev_w_uq, ev_kv_norm_w, ev_w_ukv, ev_conv_w, ev_conv_b, ev_dt_bias, ev_a_log, ev_d_skip, ev_ssm_norm_w, ev_w_out, od_w_in, od_conv_w, od_conv_b, od_i_bias, od_f_bias, od_head_norm_w, od_w_out, final_norm_w):
    bsz, l, d = x.shape
    lc = ctx.shape[1]
    depth = ada_w.shape[0]
    tm = 256 if lc % 256 == 0 else CHUNK
    assert lc % tm == 0 and l % tm == 0 and l % GRID_W == 0
    nct = lc // tm

    rows = -(-(bsz + 1) // SUBLANES) * SUBLANES
    svec = jnp.concatenate([c, c_ctx[None], jnp.zeros((rows - bsz - 1, d), F32)], axis=0)
    mods = _ada(svec, ada_w, ada_b)
    mod_ctx = jnp.broadcast_to(mods[:, bsz][:, None], (depth, bsz, 6 * d))
    mod_all = jnp.stack([mod_ctx, mods[:, :bsz]], axis=2)[:, :, :, None, :]

    cos_t, sin_t = _rope_tables(lc, l)
    xs = jnp.concatenate([ctx, x], axis=1)

    for layer in range(depth):
        mod = mod_all[layer]
        if layer % 2 == 0:
            e = layer // 2
            w_in_p, wq, wqs, wk, wva, wvb, wo_a, wo_s = _even_weights(ev_w_in[e], ev_w_uq[e], ev_w_ukv[e], ev_w_out[e])
            (p,) = _inproj(xs, norm_mix_w[layer], mod, w_in_p, tm, nct)
            q, k, va, vb = _mla_prep(p, cos_t, sin_t, ev_q_norm_w[e], ev_kv_norm_w[e], wq, wqs, wk, wva, wvb, tm)
            o_attn = _attention(q, k, va, vb, tm, lc)
            cv = _conv(p, EV_XBC, ev_conv_w[e], ev_conv_b[e], lc, l)
            s_ssd = _ssd(cv, p, ev_dt_bias[e], ev_a_log[e], ev_d_skip[e], ev_ssm_norm_w[e], lc)
            mix, mix_w = [o_attn, s_ssd], [wo_a, wo_s]
        else:
            o = layer // 2
            p, v_t = _inproj(xs, norm_mix_w[layer], mod, _odd_weights(od_w_in[o]), tm, nct,
                             tcols=(OD_V, MLSTM_VW))
            cv = _conv(p, OD_QK, od_conv_w[o], od_conv_b[o], lc, l)
            q_t = jnp.swapaxes(cv[:, :, :MLSTM_QK], 1, 2)
            m = _mlstm(cv, q_t, v_t, p, od_i_bias[o], od_f_bias[o], od_head_norm_w[o], lc)
            mix, mix_w = [m], [od_w_out[o].astype(BF16)]
        xs = _mix_mlp(mix, mix_w, xs, norm_mlp_w[layer], mod, mlp_w1[layer].astype(BF16),
                      mlp_w2[layer].astype(BF16), tm, nct, final_w=final_norm_w if layer == depth - 1 else None)
    return xs
```

```python
import functools

import jax
import jax.numpy as jnp
import numpy as np
from jax import lax
from jax.experimental import pallas as pl
from jax.experimental.pallas import tpu as pltpu

F32 = jnp.float32
BF16 = jnp.bfloat16

EPS = 1e-6
GRID_W = 64
CHUNK = 128
CONV_K = 5
LANES = 128
SUBLANES = 8

MLA_HEADS = 8
MLA_NOPE = 64
MLA_ROPE = 32
MLA_V = 64
Q_LORA = 256
KV_LORA = 256
ROPE_THETA = 10000.0
MLA_SCALE = (MLA_NOPE + MLA_ROPE) ** -0.5
HEAD_PAD = 128
LOG2E = 1.4426950408889634
ATT_SUM_LANE_A = MLA_V
ATT_SUM_LANE_B = 0

SSM_HEADS = 8
SSM_HEADDIM = 64
SSM_DINNER = SSM_HEADS * SSM_HEADDIM
SSM_GROUPS = 2
SSM_STATE = 128
SSM_XBC = SSM_DINNER + 2 * SSM_GROUPS * SSM_STATE

MLSTM_HEADS = 8
MLSTM_DQK = 64
MLSTM_DV = 128
MLSTM_QK = MLSTM_HEADS * MLSTM_DQK
MLSTM_VW = MLSTM_HEADS * MLSTM_DV

VMEM_LIMIT = 56 * 1024 * 1024

EV_CQ, EV_CKV, EV_Z, EV_XBC = 0, 256, 512, 1024
EV_ROPE_A, EV_ROPE_B, EV_DT, EV_N = 2048, 2176, 2304, 2432
OD_QK, OD_V, OD_O, OD_G, OD_N = 0, 1024, 2048, 3072, 3200


def _cparams(sem):
    return pltpu.CompilerParams(dimension_semantics=sem, vmem_limit_bytes=VMEM_LIMIT)


def _dot(a, b):
    return jnp.dot(a, b, preferred_element_type=F32)


def _dot_nt(a, b):
    return lax.dot_general(a, b, (((1,), (1,)), ((), ())), preferred_element_type=F32)


def _dot_tn(a, b):
    return lax.dot_general(a, b, (((0,), (0,)), ((), ())), preferred_element_type=F32)


def _sigmoid(x):
    return 1.0 / (1.0 + jnp.exp(-x))


def _softplus(x):
    return jnp.maximum(x, 0.0) + jnp.log(1.0 + jnp.exp(-jnp.abs(x)))


def _rms(x, w):
    ms = jnp.mean(x * x, axis=-1, keepdims=True)
    return x * lax.rsqrt(ms + EPS) * w


def _dot_f32_rows(x, m_bf16):
    hi = x.astype(BF16)
    r1 = x - hi.astype(F32)
    mid = r1.astype(BF16)
    lo = (r1 - mid.astype(F32)).astype(BF16)
    return _dot(hi, m_bf16) + _dot(mid, m_bf16) + _dot(lo, m_bf16)


def _ada_kernel(s_ref, w_ref, b_ref, o_ref):
    s = s_ref[...]
    s = (s * _sigmoid(s)).astype(BF16)
    o_ref[...] = _dot(s, w_ref[...].astype(BF16)) + b_ref[...]


def _ada(svec, ada_w, ada_b):
    depth, d, n6 = ada_w.shape
    rows = svec.shape[0]
    tn = 1024
    return pl.pallas_call(
        _ada_kernel,
        grid=(depth, n6 // tn),
        in_specs=[pl.BlockSpec((rows, d), lambda l, j: (0, 0)),
                  pl.BlockSpec((None, d, tn), lambda l, j: (l, 0, j)),
                  pl.BlockSpec((None, 1, tn), lambda l, j: (l, 0, j))],
        out_specs=pl.BlockSpec((None, rows, tn), lambda l, j: (l, 0, j)),
        out_shape=jax.ShapeDtypeStruct((depth, rows, n6), F32),
        compiler_params=_cparams(("parallel", "parallel")),
        name="ada",
    )(svec, ada_w, ada_b.reshape(depth, 1, n6))


def _inproj_kernel(x_ref, nw_ref, sh_ref, sc_ref, w_ref, o_ref, *t_refs, tcols):
    h = _rms(x_ref[...], nw_ref[...]) * (1.0 + sc_ref[...]) + sh_ref[...]
    res = _dot(h.astype(BF16), w_ref[...])
    o_ref[...] = res
    if tcols is not None:
        for j in range(res.shape[0] // CHUNK):
            blk = res[j * CHUNK:(j + 1) * CHUNK, tcols[0]:tcols[0] + tcols[1]]
            t_refs[0][j] = jnp.transpose(blk).astype(BF16)


def _mod_spec(d, chunk, nct):
    return pl.BlockSpec((None, None, 1, d), lambda b, i: (b, jnp.where(i >= nct, 1, 0), 0, chunk))


def _inproj(xs, nw, mod, w, tm, nct, tcols=None):
    bsz, t, d = xs.shape
    n = w.shape[1]
    out_specs = [pl.BlockSpec((None, tm, n), lambda b, i: (b, i, 0))]
    out_shape = [jax.ShapeDtypeStruct((bsz, t, n), F32)]
    if tcols is not None:
        out_specs.append(pl.BlockSpec((None, tm // CHUNK, tcols[1], CHUNK), lambda b, i: (b, i, 0, 0)))
        out_shape.append(jax.ShapeDtypeStruct((bsz, t // CHUNK, tcols[1], CHUNK), BF16))
    return pl.pallas_call(
        functools.partial(_inproj_kernel, tcols=tcols),
        grid=(bsz, t // tm),
        in_specs=[pl.BlockSpec((None, tm, d), lambda b, i: (b, i, 0)),
                  pl.BlockSpec((1, d), lambda b, i: (0, 0)),
                  _mod_spec(d, 0, nct), _mod_spec(d, 1, nct),
                  pl.BlockSpec((d, n), lambda b, i: (0, 0))],
        out_specs=out_specs,
        out_shape=out_shape,
        compiler_params=_cparams(("parallel", "parallel")),
        name="inproj",
    )(xs, nw.reshape(1, d), mod, mod, w)


def _conv_kernel(x_ref, w_ref, b_ref, o_ref, pad_ref, *, lc, l):
    ct = x_ref.shape[1]
    zeros = jnp.zeros((SUBLANES, ct), F32)
    pad_ref[0:SUBLANES, :] = zeros
    pad_ref[SUBLANES:SUBLANES + lc, :] = x_ref[0:lc, :]
    pad_ref[SUBLANES + lc:2 * SUBLANES + lc, :] = zeros
    pad_ref[2 * SUBLANES + lc:2 * SUBLANES + lc + l, :] = x_ref[lc:lc + l, :]
    pad_ref[2 * SUBLANES + lc + l:3 * SUBLANES + lc + l, :] = zeros
    w = w_ref[...]
    bias = b_ref[...]
    for r0 in range(0, lc + l, CHUNK):
        base = (SUBLANES if r0 < lc else 2 * SUBLANES) + r0
        acc = bias + w[0:1, :] * pad_ref[base - 2:base - 2 + CHUNK, :]
        for k in range(1, CONV_K):
            acc = acc + w[k:k + 1, :] * pad_ref[base + k - 2:base + k - 2 + CHUNK, :]
        o_ref[r0:r0 + CHUNK, :] = acc * _sigmoid(acc)


def _conv(p, col0, conv_w, conv_b, lc, l):
    bsz, t, _ = p.shape
    c = conv_w.shape[1]
    ct = 256
    cb0 = col0 // ct
    return pl.pallas_call(
        functools.partial(_conv_kernel, lc=lc, l=l),
        grid=(bsz, c // ct),
        in_specs=[pl.BlockSpec((None, t, ct), lambda b, j: (b, 0, cb0 + j)),
                  pl.BlockSpec((CONV_K, ct), lambda b, j: (0, j)),
                  pl.BlockSpec((1, ct), lambda b, j: (0, j))],
        out_specs=pl.BlockSpec((None, t, ct), lambda b, j: (b, 0, j)),
        out_shape=jax.ShapeDtypeStruct((bsz, t, c), F32),
        scratch_shapes=[pltpu.VMEM((t + 3 * SUBLANES, ct), F32)],
        compiler_params=_cparams(("parallel", "parallel")),
        name="dwconv_silu",
    )(p, conv_w, conv_b.reshape(1, c))


def _mla_prep_kernel(cq_ref, ckv_ref, ra_ref, rb_ref, cos_ref, sin_ref, qnw_ref, kvnw_ref,
                     wq_ref, wqs_ref, wk_ref, wva_ref, wvb_ref, q_ref, k_ref, va_ref, vb_ref):
    cos = cos_ref[...]
    sin = sin_ref[...]
    cos8 = jnp.concatenate([cos] * MLA_HEADS, axis=1)
    sin8 = jnp.concatenate([sin] * MLA_HEADS, axis=1)
    cqn = _rms(cq_ref[...], qnw_ref[...]).astype(BF16)
    q = _dot(cqn, wq_ref[...]) * cos8 + _dot(cqn, wqs_ref[...]) * sin8
    q_ref[...] = q.astype(BF16)
    ckvn = _rms(ckv_ref[...], kvnw_ref[...]).astype(BF16)
    kr = ra_ref[...] * cos + rb_ref[...] * sin
    k = _dot(ckvn, wk_ref[...]) + jnp.concatenate([kr] * MLA_HEADS, axis=1)
    k_ref[...] = k.astype(BF16)
    lane = lax.broadcasted_iota(jnp.int32, (1, va_ref.shape[1]), 1) % (2 * MLA_V)
    va_ref[...] = (_dot(ckvn, wva_ref[...]) + (lane == ATT_SUM_LANE_A).astype(F32)).astype(BF16)
    vb_ref[...] = (_dot(ckvn, wvb_ref[...]) + (lane == ATT_SUM_LANE_B).astype(F32)).astype(BF16)


def _mla_prep(p, cos_t, sin_t, qnw, kvnw, wq, wqs, wk, wva, wvb, tm):
    bsz, t, _ = p.shape
    hq = MLA_HEADS * HEAD_PAD
    hv = MLA_HEADS * MLA_V
    full = lambda shape: pl.BlockSpec(shape, lambda b, i: (0, 0))
    return pl.pallas_call(
        _mla_prep_kernel,
        grid=(bsz, t // tm),
        in_specs=[pl.BlockSpec((None, tm, Q_LORA), lambda b, i: (b, i, EV_CQ // Q_LORA)),
                  pl.BlockSpec((None, tm, KV_LORA), lambda b, i: (b, i, EV_CKV // KV_LORA)),
                  pl.BlockSpec((None, tm, LANES), lambda b, i: (b, i, EV_ROPE_A // LANES)),
                  pl.BlockSpec((None, tm, LANES), lambda b, i: (b, i, EV_ROPE_B // LANES)),
                  pl.BlockSpec((tm, LANES), lambda b, i: (i, 0)),
                  pl.BlockSpec((tm, LANES), lambda b, i: (i, 0)),
                  full((1, Q_LORA)), full((1, KV_LORA)),
                  full((Q_LORA, hq)), full((Q_LORA, hq)), full((KV_LORA, hq)),
                  full((KV_LORA, hv)), full((KV_LORA, hv))],
        out_specs=[pl.BlockSpec((None, tm, hq), lambda b, i: (b, i, 0)),
                   pl.BlockSpec((None, tm, hq), lambda b, i: (b, i, 0)),
                   pl.BlockSpec((None, tm, hv), lambda b, i: (b, i, 0)),
                   pl.BlockSpec((None, tm, hv), lambda b, i: (b, i, 0))],
        out_shape=[jax.ShapeDtypeStruct((bsz, t, hq), BF16), jax.ShapeDtypeStruct((bsz, t, hq), BF16),
                   jax.ShapeDtypeStruct((bsz, t, hv), BF16), jax.ShapeDtypeStruct((bsz, t, hv), BF16)],
        compiler_params=_cparams(("parallel", "parallel")),
        name="mla_prep",
    )(p, p, p, p, cos_t, sin_t, qnw.reshape(1, -1), kvnw.reshape(1, -1), wq, wqs, wk, wva, wvb)


def _attn_kernel(q_ref, k_ref, va_ref, vb_ref, o_ref, *, lc, nct):
    i = pl.program_id(2)

    def attend(nk):
        lane = lax.broadcasted_iota(jnp.int32, (1, 2 * MLA_V), 1)
        acc = None
        scores = [_dot_nt(q_ref[:, e * HEAD_PAD:(e + 1) * HEAD_PAD], k_ref[0:nk, e * HEAD_PAD:(e + 1) * HEAD_PAD])
                  for e in range(2)]
        for e, (v_ref, sum_lane) in enumerate(((va_ref, ATT_SUM_LANE_A), (vb_ref, ATT_SUM_LANE_B))):
            s = scores[e]
            m = jnp.max(s, axis=-1, keepdims=True)
            p = jnp.exp2((s - m) * (MLA_SCALE * LOG2E))
            o = _dot(p.astype(BF16), v_ref[0:nk, :])
            l = jnp.sum(jnp.where(lane == sum_lane, o, 0.0), axis=-1, keepdims=True)
            o = jnp.where((lane // MLA_V) == e, o / l, 0.0)
            acc = o if acc is None else acc + o
        o_ref[...] = acc.astype(BF16)

    @pl.when(i < nct)
    def _():
        attend(lc)

    @pl.when(i >= nct)
    def _():
        attend(k_ref.shape[0])


def _attention(q, k, va, vb, tq, lc):
    bsz, t, _ = q.shape
    npair = MLA_HEADS // 2
    return pl.pallas_call(
        functools.partial(_attn_kernel, lc=lc, nct=lc // tq),
        grid=(bsz, npair, t // tq),
        in_specs=[pl.BlockSpec((None, tq, 2 * HEAD_PAD), lambda b, p, i: (b, i, p)),
                  pl.BlockSpec((None, t, 2 * HEAD_PAD), lambda b, p, i: (b, 0, p)),
                  pl.BlockSpec((None, t, 2 * MLA_V), lambda b, p, i: (b, 0, p)),
                  pl.BlockSpec((None, t, 2 * MLA_V), lambda b, p, i: (b, 0, p))],
        out_specs=pl.BlockSpec((None, tq, 2 * MLA_V), lambda b, p, i: (b, i, p)),
        out_shape=jax.ShapeDtypeStruct((bsz, t, MLA_HEADS * MLA_V), BF16),
        compiler_params=_cparams(("parallel", "parallel", "parallel")),
        name="mla_attention",
    )(q, k, va, vb)


def _chunk_of_step(s, n, ncc):
    r = s - n
    rev = jnp.where(r < ncc, ncc - 1 - r, n - 1 - (r - ncc))
    return jnp.where(s < n, s, rev)


def _scan_masks(rev):
    ti = lax.broadcasted_iota(jnp.int32, (CHUNK, CHUNK), 0)
    si = lax.broadcasted_iota(jnp.int32, (CHUNK, CHUNK), 1)
    eye = ti == si
    if rev:
        cum_m = (ti >= si).astype(BF16)
        mask = si >= ti
        last = si[0:1, :] == 0
    else:
        cum_m = (ti <= si).astype(BF16)
        mask = si <= ti
        last = si[0:1, :] == CHUNK - 1
    return cum_m, mask, eye, last


def _to_col(row, eye):
    return jnp.sum(jnp.where(eye, row, 0.0), axis=1, keepdims=True)


def _pick_last(row, last):
    return jnp.sum(jnp.where(last, row, 0.0), axis=1, keepdims=True)


def _ssd_kernel(xs_ref, bm_ref, cm_ref, z_ref, dt_ref, dtb_ref, alog_ref, dskip_ref, nw_ref,
                o_ref, ybuf, ht, *, n, ncc):
    s = pl.program_id(1)
    c = _chunk_of_step(s, n, ncc)
    hpg = SSM_HEADS // SSM_GROUPS
    gw = hpg * SSM_HEADDIM

    @pl.when(jnp.logical_or(s == 0, s == n))
    def _():
        ht[...] = jnp.zeros_like(ht)

    def chunk(rev):
        d = 1 if rev else 0
        cum_m, mask, eye, last = _scan_masks(rev)
        rows = slice(d * SSM_HEADS, (d + 1) * SSM_HEADS)
        dt_rows = jnp.transpose(dt_ref[...])
        dt = _softplus(dt_rows[rows, :] + dtb_ref[rows, :])
        da = dt * (-jnp.exp(alog_ref[rows, :]))
        cum = _dot_f32_rows(da, cum_m)
        head_of_lane = lax.broadcasted_iota(jnp.int32, (CHUNK, gw), 1) // SSM_HEADDIM
        xs = xs_ref[...]
        cols = [_to_col(cum[h:h + 1, :], eye) for h in range(SSM_HEADS)]
        dtcols = [_to_col(dt[h:h + 1, :], eye) for h in range(SSM_HEADS)]
        tots = [_pick_last(cum[h:h + 1, :], last) for h in range(SSM_HEADS)]
        bmbs, cmbs, cbs, xgs, xgbs, htgs, inters = [], [], [], [], [], [], []
        for g in range(SSM_GROUPS):
            bmbs.append(bm_ref[:, g * SSM_STATE:(g + 1) * SSM_STATE].astype(BF16))
            cmbs.append(cm_ref[:, g * SSM_STATE:(g + 1) * SSM_STATE].astype(BF16))
            cbs.append(_dot_nt(cmbs[g], bmbs[g]))
            xgs.append(xs[:, g * gw:(g + 1) * gw])
            xgbs.append(xgs[g].astype(BF16))
            htgs.append(ht[g])
            inters.append(_dot(cmbs[g], htgs[g].astype(BF16)))
        ys = []
        for g in range(SSM_GROUPS):
            yg = jnp.zeros((CHUNK, gw), F32)
            eg = jnp.zeros((CHUNK, gw), F32)
            teg = jnp.zeros((CHUNK, gw), F32)
            dg = jnp.zeros((1, gw), F32)
            for r in range(hpg):
                h = g * hpg + r
                row = cum[h:h + 1, :]
                dtrow = dt[h:h + 1, :]
                col, dtcol, tot = cols[h], dtcols[h], tots[h]
                sel = head_of_lane == r
                eg = jnp.where(sel, jnp.exp(col), eg)
                teg = jnp.where(sel, jnp.exp(tot - col) * dtcol, teg)
                dg = jnp.where(sel[0:1, :], jnp.exp(tot), dg)
                decay = jnp.exp(jnp.where(mask, col - row, -jnp.inf))
                w = cbs[g] * decay * dtrow
                yh = _dot(w.astype(BF16), xgbs[g])
                yg = jnp.where(sel, yh, yg)
            ht[g] = htgs[g] * dg + _dot_tn(bmbs[g], (xgs[g] * teg).astype(BF16))
            ys.append(yg + inters[g] * eg)
        return jnp.concatenate(ys, axis=1), xs

    @pl.when(s < n)
    def _():
        y, xs = chunk(False)
        ybuf[c] = y + dskip_ref[...] * xs

    @pl.when(s >= n)
    def _():
        y, _ = chunk(True)
        y = y + ybuf[c]
        z = z_ref[...]
        gated = y * (z * _sigmoid(z))
        outs = []
        for g in range(SSM_GROUPS):
            gg = gated[:, g * gw:(g + 1) * gw]
            outs.append(gg * lax.rsqrt(jnp.mean(gg * gg, axis=-1, keepdims=True) + EPS))
        o_ref[...] = (jnp.concatenate(outs, axis=1) * nw_ref[...]).astype(BF16)


def _ssd(cv, p, dt_bias, a_log, d_skip, norm_w, lc):
    bsz, t, _ = cv.shape
    n = t // CHUNK
    ncc = lc // CHUNK
    cidx = lambda s: _chunk_of_step(s, n, ncc)
    cidx_out = lambda s: _chunk_of_step(jnp.maximum(s, n), n, ncc)
    nst = SSM_GROUPS * SSM_STATE
    small = lambda shape: pl.BlockSpec(shape, lambda b, s: (0, 0))
    return pl.pallas_call(
        functools.partial(_ssd_kernel, n=n, ncc=ncc),
        grid=(bsz, 2 * n),
        in_specs=[pl.BlockSpec((None, CHUNK, SSM_DINNER), lambda b, s: (b, cidx(s), 0)),
                  pl.BlockSpec((None, CHUNK, nst), lambda b, s: (b, cidx(s), SSM_DINNER // nst)),
                  pl.BlockSpec((None, CHUNK, nst), lambda b, s: (b, cidx(s), SSM_DINNER // nst + 1)),
                  pl.BlockSpec((None, CHUNK, SSM_DINNER), lambda b, s: (b, cidx_out(s), EV_Z // SSM_DINNER)),
                  pl.BlockSpec((None, CHUNK, LANES), lambda b, s: (b, cidx(s), EV_DT // LANES)),
                  small((2 * SSM_HEADS, 1)), small((2 * SSM_HEADS, 1)),
                  small((1, SSM_DINNER)), small((1, SSM_DINNER))],
        out_specs=pl.BlockSpec((None, CHUNK, SSM_DINNER), lambda b, s: (b, cidx_out(s), 0)),
        out_shape=jax.ShapeDtypeStruct((bsz, t, SSM_DINNER), BF16),
        scratch_shapes=[pltpu.VMEM((n, CHUNK, SSM_DINNER), F32),
                        pltpu.VMEM((SSM_GROUPS, SSM_STATE, SSM_DINNER // SSM_GROUPS), F32)],
        compiler_params=_cparams(("parallel", "arbitrary")),
        name="ssd_scan",
    )(cv, cv, cv, p, p, dt_bias.reshape(-1, 1), a_log.reshape(-1, 1),
      jnp.repeat(d_skip, SSM_HEADDIM).reshape(1, -1), norm_w.reshape(1, -1))


MLSTM_XR = 16


def _mlstm_kernel(k_ref, qt_ref, vt_ref, o_ref, g_ref, ib_ref, fb_ref, nw_ref,
                  out_ref, hbuf, cst, mst, *, n, ncc):
    s = pl.program_id(1)
    c = _chunk_of_step(s, n, ncc)
    nh, dk, dv = MLSTM_HEADS, MLSTM_DQK, MLSTM_DV
    pw = 2 * dk

    @pl.when(jnp.logical_or(s == 0, s == n))
    def _():
        cst[...] = jnp.zeros_like(cst)
        mst[...] = jnp.zeros_like(mst)

    def chunk(rev):
        d = 1 if rev else 0
        si = lax.broadcasted_iota(jnp.int32, (CHUNK, CHUNK), 0)
        ti = lax.broadcasted_iota(jnp.int32, (CHUNK, CHUNK), 1)
        seen = (si >= ti) if rev else (si <= ti)
        cum_m = seen.astype(BF16)
        g_rows = jnp.transpose(g_ref[...])
        ipre = g_rows[d * nh:(d + 1) * nh, :] + ib_ref[d * nh:(d + 1) * nh, :]
        fpre = g_rows[(2 + d) * nh:(3 + d) * nh, :] + fb_ref[d * nh:(d + 1) * nh, :]
        bcum = _dot_f32_rows(-_softplus(-fpre), cum_m)
        lane = lax.broadcasted_iota(jnp.int32, (1, CHUNK), 1)
        btot = jnp.sum(jnp.where(lane == (0 if rev else CHUNK - 1), bcum, 0.0), axis=1, keepdims=True)
        cs = ipre - bcum
        wend = btot + cs
        m_in = mst[:, 0:1]
        m_new = jnp.maximum(btot + m_in, jnp.max(wend, axis=1, keepdims=True))
        keep = jnp.exp(btot + m_in - m_new)
        wt = jnp.exp(wend - m_new)
        mst[...] = jnp.broadcast_to(m_new, mst.shape)

        row_half = lax.broadcasted_iota(jnp.int32, (pw, CHUNK), 0) // dk
        lane_half = lax.broadcasted_iota(jnp.int32, (CHUNK, pw), 1) // dk
        ones_rows = (lax.broadcasted_iota(jnp.int32, (MLSTM_XR, CHUNK), 0) == 0).astype(BF16)
        zblk = jnp.zeros((CHUNK, CHUNK), BF16)
        outs = []
        sts, inters, vtps = [], [], []
        for p in range(nh // 2):
            kp = k_ref[:, p * pw:(p + 1) * pw] * (dk ** -0.5)
            qtp = qt_ref[p * pw:(p + 1) * pw, :]
            qbd = jnp.concatenate([jnp.where(row_half == 0, qtp, 0.0), jnp.where(row_half == 1, qtp, 0.0)],
                                  axis=1).astype(BF16)
            sts.append(_dot(kp.astype(BF16), qbd))
            cp = cst[p]
            inters.append(_dot(cp.astype(BF16), qbd))
            vtp = jnp.concatenate(
                [jnp.concatenate([vt_ref[(2 * p + e) * dv:(2 * p + e + 1) * dv, :], ones_rows], axis=0)
                 for e in range(2)], axis=1)
            vtps.append(vtp.astype(BF16))
            wtp = jnp.concatenate([wt[2 * p:2 * p + 1, :], wt[2 * p + 1:2 * p + 2, :]], axis=1)
            kbd = jnp.concatenate([jnp.where(lane_half == 0, kp, 0.0), jnp.where(lane_half == 1, kp, 0.0)],
                                  axis=0).astype(BF16)
            keepp = jnp.where(lane_half[0:1, :] == 0, keep[2 * p:2 * p + 1, :], keep[2 * p + 1:2 * p + 2, :])
            cst[p] = cp * keepp + _dot((vtp * wtp).astype(BF16), kbd)
        for p in range(nh // 2):
            st = sts[p]
            ats, keepqs, mrows = [], [], []
            for e in range(2):
                h = 2 * p + e
                cs_st = jnp.transpose(jnp.broadcast_to(cs[h:h + 1, :], (CHUNK, CHUNK)))
                ex = jnp.where(seen, cs_st, -jnp.inf)
                mrow = jnp.maximum(m_in[h:h + 1, :], jnp.max(ex, axis=0, keepdims=True))
                ats.append((st[:, e * CHUNK:(e + 1) * CHUNK] * jnp.exp(ex - mrow)).astype(BF16))
                keepqs.append(jnp.exp(m_in[h:h + 1, :] - mrow))
                mrows.append(mrow)
            abd = jnp.concatenate([jnp.concatenate([ats[0], zblk], axis=1),
                                   jnp.concatenate([zblk, ats[1]], axis=1)], axis=0)
            nd = _dot(vtps[p], abd) + inters[p] * jnp.concatenate(keepqs, axis=1)
            for e in range(2):
                h = 2 * p + e
                num = nd[0:dv, e * CHUNK:(e + 1) * CHUNK]
                den = nd[dv:dv + 1, e * CHUNK:(e + 1) * CHUNK]
                mt = bcum[h:h + 1, :] + mrows[e]
                outs.append(num * (1.0 / jnp.maximum(jnp.abs(den), jnp.exp(-mt))))
        return jnp.concatenate(outs, axis=0)

    @pl.when(s < n)
    def _():
        hbuf[c] = chunk(False)

    @pl.when(s >= n)
    def _():
        ht = chunk(True) + hbuf[c]
        outs = []
        for h in range(nh):
            hh = ht[h * dv:(h + 1) * dv, :]
            hn = hh * lax.rsqrt(jnp.mean(hh * hh, axis=0, keepdims=True) + EPS)
            outs.append(jnp.transpose(hn))
        hn = jnp.concatenate(outs, axis=1) * nw_ref[...]
        out_ref[...] = (_sigmoid(o_ref[...]) * hn).astype(BF16)


def _mlstm(cv, q_t, v_t, p, i_bias, f_bias, head_norm_w, lc):
    bsz, t, _ = cv.shape
    n = t // CHUNK
    ncc = lc // CHUNK
    cidx = lambda s: _chunk_of_step(s, n, ncc)
    cidx_out = lambda s: _chunk_of_step(jnp.maximum(s, n), n, ncc)
    nh = MLSTM_HEADS
    small = lambda shape: pl.BlockSpec(shape, lambda b, s: (0, 0))
    return pl.pallas_call(
        functools.partial(_mlstm_kernel, n=n, ncc=ncc),
        grid=(bsz, 2 * n),
        in_specs=[pl.BlockSpec((None, CHUNK, MLSTM_QK), lambda b, s: (b, cidx(s), 1)),
                  pl.BlockSpec((None, None, MLSTM_QK, CHUNK), lambda b, s: (b, cidx(s), 0, 0)),
                  pl.BlockSpec((None, None, MLSTM_VW, CHUNK), lambda b, s: (b, cidx(s), 0, 0)),
                  pl.BlockSpec((None, CHUNK, MLSTM_VW), lambda b, s: (b, cidx_out(s), OD_O // MLSTM_VW)),
                  pl.BlockSpec((None, CHUNK, LANES), lambda b, s: (b, cidx(s), OD_G // LANES)),
                  small((2 * nh, 1)), small((2 * nh, 1)), small((1, MLSTM_VW))],
        out_specs=pl.BlockSpec((None, CHUNK, MLSTM_VW), lambda b, s: (b, cidx_out(s), 0)),
        out_shape=jax.ShapeDtypeStruct((bsz, t, MLSTM_VW), BF16),
        scratch_shapes=[pltpu.VMEM((n, MLSTM_VW, CHUNK), F32),
                        pltpu.VMEM((nh // 2, MLSTM_DV + MLSTM_XR, 2 * MLSTM_DQK), F32),
                        pltpu.VMEM((nh, LANES), F32)],
        compiler_params=_cparams(("parallel", "arbitrary")),
        name="mlstm_scan",
    )(cv, q_t, v_t, p, p, i_bias.reshape(-1, 1), f_bias.reshape(-1, 1), head_norm_w.reshape(1, -1))


def _mix_mlp_kernel(*refs, nin, tf):
    ms = refs[:nin]
    ws = refs[nin:2 * nin]
    x_ref, g1_ref, nw_ref, sh_ref, sc_ref, g2_ref, w1_ref, w2_ref, fw_ref, o_ref = refs[2 * nin:]
    mix = _dot(ms[0][...], ws[0][...])
    for m_ref, w_ref in zip(ms[1:], ws[1:]):
        mix = mix + _dot(m_ref[...], w_ref[...])
    x = x_ref[...] + g1_ref[...] * mix
    h = (_rms(x, nw_ref[...]) * (1.0 + sc_ref[...]) + sh_ref[...]).astype(BF16)
    acc = jnp.zeros(x.shape, F32)
    for f in range(0, w1_ref.shape[1], tf):
        u = jnp.maximum(_dot(h, w1_ref[:, f:f + tf]), 0.0)
        acc = acc + _dot((u * u).astype(BF16), w2_ref[f:f + tf, :])
    x = x + g2_ref[...] * acc
    o_ref[...] = x if fw_ref is None else _rms(x, fw_ref[...])


def _mix_mlp(ms, ws, xs, nw, mod, w1, w2, tm, nct, final_w=None):
    bsz, t, d = xs.shape
    dff = w1.shape[1]
    nin = len(ms)
    skip = nct if final_w is not None else 0
    rows = lambda width: pl.BlockSpec((None, tm, width), lambda b, i: (b, i + skip, 0))
    full = lambda shape: pl.BlockSpec(shape, lambda b, i: (0, 0))
    mspec = lambda chunk: _mod_spec(d, chunk, nct - skip)
    args = [*ms, *ws, xs, mod, nw.reshape(1, d), mod, mod, mod, w1, w2]
    in_specs = ([rows(m.shape[2]) for m in ms] + [full(w.shape) for w in ws]
                + [rows(d), mspec(2), full((1, d)), mspec(3), mspec(4), mspec(5), full((d, dff)), full((dff, d))])
    kern = functools.partial(_mix_mlp_kernel, nin=nin, tf=1024)
    if final_w is not None:
        args.append(final_w.reshape(1, d))
        in_specs.append(full((1, d)))
    else:
        kern = functools.partial(_no_final, kern, len(args))
    return pl.pallas_call(
        kern,
        grid=(bsz, t // tm - skip),
        in_specs=in_specs,
        out_specs=pl.BlockSpec((None, tm, d), lambda b, i: (b, i, 0)),
        out_shape=jax.ShapeDtypeStruct((bsz, t - skip * tm, d), F32),
        compiler_params=_cparams(("parallel", "parallel")),
        name="outproj_mlp_residual",
    )(*args)


def _no_final(kern, nargs, *refs):
    kern(*refs[:nargs], None, *refs[nargs:])


def _rope_swap(w):
    j = np.arange(MLA_ROPE)
    first = (j % 16) < 8
    src = np.where(first, j + 8, j - 8)
    sign = np.where(first, -1.0, 1.0).astype(np.float32)
    return w[..., src] * sign


def _rope_tables(lc, l):
    n_rows = l // GRID_W
    row = jnp.repeat(jnp.arange(n_rows), GRID_W).astype(F32)
    col = jnp.tile(jnp.arange(GRID_W), n_rows).astype(F32)
    half = MLA_ROPE // 2
    inv = 1.0 / (ROPE_THETA ** (jnp.arange(0, half, 2, dtype=F32) / half))
    ar = row[:, None] * inv
    ac = col[:, None] * inv
    cos_l = jnp.concatenate([jnp.ones((l, MLA_NOPE), F32), jnp.cos(ar), jnp.cos(ar), jnp.cos(ac), jnp.cos(ac),
                             jnp.ones((l, HEAD_PAD - MLA_NOPE - MLA_ROPE), F32)], axis=1)
    sin_l = jnp.concatenate([jnp.zeros((l, MLA_NOPE), F32), jnp.sin(ar), jnp.sin(ar), jnp.sin(ac), jnp.sin(ac),
                             jnp.zeros((l, HEAD_PAD - MLA_NOPE - MLA_ROPE), F32)], axis=1)
    cos_t = jnp.concatenate([jnp.ones((lc, HEAD_PAD), F32), cos_l], axis=0)
    sin_t = jnp.concatenate([jnp.zeros((lc, HEAD_PAD), F32), sin_l], axis=0)
    return cos_t, sin_t


def _even_weights(w_in, w_uq, w_ukv, w_out):
    d = w_in.shape[0]
    o = np.cumsum([0, Q_LORA, KV_LORA, MLA_ROPE, SSM_DINNER, SSM_XBC, 2 * SSM_HEADS])
    cq, ckv, kr, z, xbc, dt = (w_in[:, o[i]:o[i + 1]] for i in range(6))
    zc = lambda n: jnp.zeros((d, n), F32)
    w_in_p = jnp.concatenate(
        [cq, ckv, z, xbc,
         zc(MLA_NOPE), kr, zc(HEAD_PAD - MLA_NOPE - MLA_ROPE),
         zc(MLA_NOPE), _rope_swap(kr), zc(HEAD_PAD - MLA_NOPE - MLA_ROPE),
         dt, zc(LANES - 2 * SSM_HEADS)], axis=1).astype(BF16)
    assert w_in_p.shape[1] == EV_N
    uq = w_uq.reshape(Q_LORA, MLA_HEADS, MLA_NOPE + MLA_ROPE)
    q_nope, q_rope = uq[..., :MLA_NOPE], uq[..., MLA_NOPE:]
    zq = lambda n: jnp.zeros((Q_LORA, MLA_HEADS, n), F32)
    npad = HEAD_PAD - MLA_NOPE - MLA_ROPE
    wq = jnp.concatenate([q_nope, q_rope, zq(npad)], axis=-1).reshape(Q_LORA, -1).astype(BF16)
    wqs = jnp.concatenate([zq(MLA_NOPE), _rope_swap(q_rope), zq(npad)], axis=-1).reshape(Q_LORA, -1).astype(BF16)
    ukv = w_ukv.reshape(KV_LORA, MLA_HEADS, MLA_NOPE + MLA_V)
    k_nope, vw = ukv[..., :MLA_NOPE], ukv[..., MLA_NOPE:]
    wk = jnp.concatenate([k_nope, jnp.zeros((KV_LORA, MLA_HEADS, HEAD_PAD - MLA_NOPE), F32)], axis=-1)
    wk = wk.reshape(KV_LORA, -1).astype(BF16)
    vpair = vw.reshape(KV_LORA, MLA_HEADS // 2, 2, MLA_V)
    zv = jnp.zeros((KV_LORA, MLA_HEADS // 2, MLA_V), F32)
    wva = jnp.concatenate([vpair[:, :, 0], zv], axis=-1).reshape(KV_LORA, -1).astype(BF16)
    wvb = jnp.concatenate([zv, vpair[:, :, 1]], axis=-1).reshape(KV_LORA, -1).astype(BF16)
    hv = MLA_HEADS * MLA_V
    return w_in_p, wq, wqs, wk, wva, wvb, w_out[:hv].astype(BF16), w_out[hv:].astype(BF16)


def _odd_weights(w_in):
    d = w_in.shape[0]
    ngate = 4 * MLSTM_HEADS
    w = jnp.concatenate([w_in, jnp.zeros((d, OD_N - OD_G - ngate), F32)], axis=1).astype(BF16)
    assert w.shape[1] == OD_N
    return w


def kernel(x, c, ctx, c_ctx, ada_w, ada_b, norm_mix_w, norm_mlp_w, mlp_w1, mlp_w2, ev_w_in, ev_q_norm_w, ev_w_uq, ev_kv_norm_w, ev_w_ukv, ev_conv_w, ev_conv_b, ev_dt_bias, ev_a_log, ev_d_skip, ev_ssm_norm_w, ev_w_out, od_w_in, od_conv_w, od_conv_b, od_i_bias, od_f_bias, od_head_norm_w, od_w_out, final_norm_w):
    bsz, l, d = x.shape
    lc = ctx.shape[1]
    depth = ada_w.shape[0]
    tm = 256 if lc % 256 == 0 else CHUNK
    assert lc % tm == 0 and l % tm == 0 and l % GRID_W == 0
    nct = lc // tm

    rows = -(-(bsz + 1) // SUBLANES) * SUBLANES
    svec = jnp.concatenate([c, c_ctx[None], jnp.zeros((rows - bsz - 1, d), F32)], axis=0)
    mods = _ada(svec, ada_w, ada_b)
    mod_ctx = jnp.broadcast_to(mods[:, bsz][:, None], (depth, bsz, 6 * d))
    mod_all = jnp.stack([mod_ctx, mods[:, :bsz]], axis=2)[:, :, :, None, :]

    cos_t, sin_t = _rope_tables(lc, l)
    xs = jnp.concatenate([ctx, x], axis=1)

    for layer in range(depth):
        mod = mod_all[layer]
        if layer % 2 == 0:
            e = layer // 2
            w_in_p, wq, wqs, wk, wva, wvb, wo_a, wo_s = _even_weights(ev_w_in[e], ev_w_uq[e], ev_w_ukv[e], ev_w_out[e])
            (p,) = _inproj(xs, norm_mix_w[layer], mod, w_in_p, tm, nct)
            q, k, va, vb = _mla_prep(p, cos_t, sin_t, ev_q_norm_w[e], ev_kv_norm_w[e], wq, wqs, wk, wva, wvb, tm)
            o_attn = _attention(q, k, va, vb, tm, lc)
            cv = _conv(p, EV_XBC, ev_conv_w[e], ev_conv_b[e], lc, l)
            s_ssd = _ssd(cv, p, ev_dt_bias[e], ev_a_log[e], ev_d_skip[e], ev_ssm_norm_w[e], lc)
            mix, mix_w = [o_attn, s_ssd], [wo_a, wo_s]
        else:
            o = layer // 2
            p, v_t = _inproj(xs, norm_mix_w[layer], mod, _odd_weights(od_w_in[o]), tm, nct,
                             tcols=(OD_V, MLSTM_VW))
            cv = _conv(p, OD_QK, od_conv_w[o], od_conv_b[o], lc, l)
            q_t = jnp.swapaxes(cv[:, :, :MLSTM_QK].astype(BF16).reshape(bsz, -1, CHUNK, MLSTM_QK), 2, 3)
            m = _mlstm(cv, q_t, v_t, p, od_i_bias[o], od_f_bias[o], od_head_norm_w[o], lc)
            mix, mix_w = [m], [od_w_out[o].astype(BF16)]
        xs = _mix_mlp(mix, mix_w, xs, norm_mlp_w[layer], mod, mlp_w1[layer].astype(BF16),
                      mlp_w2[layer].astype(BF16), tm, nct, final_w=final_norm_w if layer == depth - 1 else None)
    return xs
```

```python
import functools

import jax
import jax.numpy as jnp
import numpy as np
from jax import lax
from jax.experimental import pallas as pl
from jax.experimental.pallas import tpu as pltpu

F32 = jnp.float32
BF16 = jnp.bfloat16

EPS = 1e-6
GRID_W = 64
CHUNK = 128
SCAN_SUB = 2
CONV_K = 5
LANES = 128
SUBLANES = 8

MLA_HEADS = 8
MLA_NOPE = 64
MLA_ROPE = 32
MLA_V = 64
Q_LORA = 256
KV_LORA = 256
ROPE_THETA = 10000.0
MLA_SCALE = (MLA_NOPE + MLA_ROPE) ** -0.5
HEAD_PAD = 128
LOG2E = 1.4426950408889634
ATT_SUM_LANE_A = MLA_V
ATT_SUM_LANE_B = 0

SSM_HEADS = 8
SSM_HEADDIM = 64
SSM_DINNER = SSM_HEADS * SSM_HEADDIM
SSM_GROUPS = 2
SSM_STATE = 128
SSM_XBC = SSM_DINNER + 2 * SSM_GROUPS * SSM_STATE

MLSTM_HEADS = 8
MLSTM_DQK = 64
MLSTM_DV = 128
MLSTM_QK = MLSTM_HEADS * MLSTM_DQK
MLSTM_VW = MLSTM_HEADS * MLSTM_DV

VMEM_LIMIT = 56 * 1024 * 1024

EV_CQ, EV_CKV, EV_Z, EV_XBC = 0, 256, 512, 1024
EV_ROPE_A, EV_ROPE_B, EV_DT, EV_N = 2048, 2176, 2304, 2432
OD_QK, OD_V, OD_O, OD_G, OD_N = 0, 1024, 2048, 3072, 3200


def _cparams(sem):
    return pltpu.CompilerParams(dimension_semantics=sem, vmem_limit_bytes=VMEM_LIMIT)


def _dot(a, b):
    return jnp.dot(a, b, preferred_element_type=F32)


def _dot_nt(a, b):
    return lax.dot_general(a, b, (((1,), (1,)), ((), ())), preferred_element_type=F32)


def _dot_tn(a, b):
    return lax.dot_general(a, b, (((0,), (0,)), ((), ())), preferred_element_type=F32)


def _sigmoid(x):
    return 1.0 / (1.0 + jnp.exp(-x))


def _softplus(x):
    return jnp.maximum(x, 0.0) + jnp.log(1.0 + jnp.exp(-jnp.abs(x)))


def _rms(x, w):
    ms = jnp.mean(x * x, axis=-1, keepdims=True)
    return x * lax.rsqrt(ms + EPS) * w


def _dot_f32_rows(x, m_bf16):
    hi = x.astype(BF16)
    r1 = x - hi.astype(F32)
    mid = r1.astype(BF16)
    lo = (r1 - mid.astype(F32)).astype(BF16)
    return _dot(hi, m_bf16) + _dot(mid, m_bf16) + _dot(lo, m_bf16)


def _ada_kernel(s_ref, w_ref, b_ref, o_ref):
    s = s_ref[...]
    s = (s * _sigmoid(s)).astype(BF16)
    o_ref[...] = _dot(s, w_ref[...].astype(BF16)) + b_ref[...]


def _ada(svec, ada_w, ada_b):
    depth, d, n6 = ada_w.shape
    rows = svec.shape[0]
    tn = 1024
    return pl.pallas_call(
        _ada_kernel,
        grid=(depth, n6 // tn),
        in_specs=[pl.BlockSpec((rows, d), lambda l, j: (0, 0)),
                  pl.BlockSpec((None, d, tn), lambda l, j: (l, 0, j)),
                  pl.BlockSpec((None, 1, tn), lambda l, j: (l, 0, j))],
        out_specs=pl.BlockSpec((None, rows, tn), lambda l, j: (l, 0, j)),
        out_shape=jax.ShapeDtypeStruct((depth, rows, n6), F32),
        compiler_params=_cparams(("parallel", "parallel")),
        name="ada",
    )(svec, ada_w, ada_b.reshape(depth, 1, n6))


def _inproj_kernel(x_ref, nw_ref, sh_ref, sc_ref, w_ref, o_ref, *t_refs, tcols):
    h = _rms(x_ref[...], nw_ref[...]) * (1.0 + sc_ref[...]) + sh_ref[...]
    res = _dot(h.astype(BF16), w_ref[...])
    o_ref[...] = res
    if tcols is not None:
        for j in range(res.shape[0] // CHUNK):
            blk = res[j * CHUNK:(j + 1) * CHUNK, tcols[0]:tcols[0] + tcols[1]]
            t_refs[0][j] = jnp.transpose(blk).astype(BF16)


def _mod_spec(d, chunk, nct):
    return pl.BlockSpec((None, None, 1, d), lambda b, i: (b, jnp.where(i >= nct, 1, 0), 0, chunk))


def _inproj(xs, nw, mod, w, tm, nct, tcols=None):
    bsz, t, d = xs.shape
    n = w.shape[1]
    out_specs = [pl.BlockSpec((None, tm, n), lambda b, i: (b, i, 0))]
    out_shape = [jax.ShapeDtypeStruct((bsz, t, n), F32)]
    if tcols is not None:
        out_specs.append(pl.BlockSpec((None, tm // CHUNK, tcols[1], CHUNK), lambda b, i: (b, i, 0, 0)))
        out_shape.append(jax.ShapeDtypeStruct((bsz, t // CHUNK, tcols[1], CHUNK), BF16))
    return pl.pallas_call(
        functools.partial(_inproj_kernel, tcols=tcols),
        grid=(bsz, t // tm),
        in_specs=[pl.BlockSpec((None, tm, d), lambda b, i: (b, i, 0)),
                  pl.BlockSpec((1, d), lambda b, i: (0, 0)),
                  _mod_spec(d, 0, nct), _mod_spec(d, 1, nct),
                  pl.BlockSpec((d, n), lambda b, i: (0, 0))],
        out_specs=out_specs,
        out_shape=out_shape,
        compiler_params=_cparams(("parallel", "parallel")),
        name="inproj",
    )(xs, nw.reshape(1, d), mod, mod, w)


def _conv_kernel(x_ref, w_ref, b_ref, o_ref, pad_ref, *, lc, l):
    ct = x_ref.shape[1]
    zeros = jnp.zeros((SUBLANES, ct), F32)
    pad_ref[0:SUBLANES, :] = zeros
    pad_ref[SUBLANES:SUBLANES + lc, :] = x_ref[0:lc, :]
    pad_ref[SUBLANES + lc:2 * SUBLANES + lc, :] = zeros
    pad_ref[2 * SUBLANES + lc:2 * SUBLANES + lc + l, :] = x_ref[lc:lc + l, :]
    pad_ref[2 * SUBLANES + lc + l:3 * SUBLANES + lc + l, :] = zeros
    w = w_ref[...]
    bias = b_ref[...]
    for r0 in range(0, lc + l, CHUNK):
        base = (SUBLANES if r0 < lc else 2 * SUBLANES) + r0
        acc = bias + w[0:1, :] * pad_ref[base - 2:base - 2 + CHUNK, :]
        for k in range(1, CONV_K):
            acc = acc + w[k:k + 1, :] * pad_ref[base + k - 2:base + k - 2 + CHUNK, :]
        o_ref[r0:r0 + CHUNK, :] = acc * _sigmoid(acc)


def _conv(p, col0, conv_w, conv_b, lc, l):
    bsz, t, _ = p.shape
    c = conv_w.shape[1]
    ct = 256
    cb0 = col0 // ct
    return pl.pallas_call(
        functools.partial(_conv_kernel, lc=lc, l=l),
        grid=(bsz, c // ct),
        in_specs=[pl.BlockSpec((None, t, ct), lambda b, j: (b, 0, cb0 + j)),
                  pl.BlockSpec((CONV_K, ct), lambda b, j: (0, j)),
                  pl.BlockSpec((1, ct), lambda b, j: (0, j))],
        out_specs=pl.BlockSpec((None, t, ct), lambda b, j: (b, 0, j)),
        out_shape=jax.ShapeDtypeStruct((bsz, t, c), F32),
        scratch_shapes=[pltpu.VMEM((t + 3 * SUBLANES, ct), F32)],
        compiler_params=_cparams(("parallel", "parallel")),
        name="dwconv_silu",
    )(p, conv_w, conv_b.reshape(1, c))


def _mla_prep_kernel(cq_ref, ckv_ref, ra_ref, rb_ref, cos_ref, sin_ref, qnw_ref, kvnw_ref,
                     wq_ref, wqs_ref, wk_ref, wva_ref, wvb_ref, q_ref, k_ref, va_ref, vb_ref):
    cos = cos_ref[...]
    sin = sin_ref[...]
    cos8 = jnp.concatenate([cos] * MLA_HEADS, axis=1)
    sin8 = jnp.concatenate([sin] * MLA_HEADS, axis=1)
    cqn = _rms(cq_ref[...], qnw_ref[...]).astype(BF16)
    q = _dot(cqn, wq_ref[...]) * cos8 + _dot(cqn, wqs_ref[...]) * sin8
    q_ref[...] = q.astype(BF16)
    ckvn = _rms(ckv_ref[...], kvnw_ref[...]).astype(BF16)
    kr = ra_ref[...] * cos + rb_ref[...] * sin
    k = _dot(ckvn, wk_ref[...]) + jnp.concatenate([kr] * MLA_HEADS, axis=1)
    k_ref[...] = k.astype(BF16)
    lane = lax.broadcasted_iota(jnp.int32, (1, va_ref.shape[1]), 1) % (2 * MLA_V)
    va_ref[...] = (_dot(ckvn, wva_ref[...]) + (lane == ATT_SUM_LANE_A).astype(F32)).astype(BF16)
    vb_ref[...] = (_dot(ckvn, wvb_ref[...]) + (lane == ATT_SUM_LANE_B).astype(F32)).astype(BF16)


def _mla_prep(p, cos_t, sin_t, qnw, kvnw, wq, wqs, wk, wva, wvb, tm):
    bsz, t, _ = p.shape
    hq = MLA_HEADS * HEAD_PAD
    hv = MLA_HEADS * MLA_V
    full = lambda shape: pl.BlockSpec(shape, lambda b, i: (0, 0))
    return pl.pallas_call(
        _mla_prep_kernel,
        grid=(bsz, t // tm),
        in_specs=[pl.BlockSpec((None, tm, Q_LORA), lambda b, i: (b, i, EV_CQ // Q_LORA)),
                  pl.BlockSpec((None, tm, KV_LORA), lambda b, i: (b, i, EV_CKV // KV_LORA)),
                  pl.BlockSpec((None, tm, LANES), lambda b, i: (b, i, EV_ROPE_A // LANES)),
                  pl.BlockSpec((None, tm, LANES), lambda b, i: (b, i, EV_ROPE_B // LANES)),
                  pl.BlockSpec((tm, LANES), lambda b, i: (i, 0)),
                  pl.BlockSpec((tm, LANES), lambda b, i: (i, 0)),
                  full((1, Q_LORA)), full((1, KV_LORA)),
                  full((Q_LORA, hq)), full((Q_LORA, hq)), full((KV_LORA, hq)),
                  full((KV_LORA, hv)), full((KV_LORA, hv))],
        out_specs=[pl.BlockSpec((None, tm, hq), lambda b, i: (b, i, 0)),
                   pl.BlockSpec((None, tm, hq), lambda b, i: (b, i, 0)),
                   pl.BlockSpec((None, tm, hv), lambda b, i: (b, i, 0)),
                   pl.BlockSpec((None, tm, hv), lambda b, i: (b, i, 0))],
        out_shape=[jax.ShapeDtypeStruct((bsz, t, hq), BF16), jax.ShapeDtypeStruct((bsz, t, hq), BF16),
                   jax.ShapeDtypeStruct((bsz, t, hv), BF16), jax.ShapeDtypeStruct((bsz, t, hv), BF16)],
        compiler_params=_cparams(("parallel", "parallel")),
        name="mla_prep",
    )(p, p, p, p, cos_t, sin_t, qnw.reshape(1, -1), kvnw.reshape(1, -1), wq, wqs, wk, wva, wvb)


def _attn_kernel(q_ref, k_ref, va_ref, vb_ref, o_ref, *, lc, nct):
    i = pl.program_id(2)

    def attend(nk):
        lane = lax.broadcasted_iota(jnp.int32, (1, 2 * MLA_V), 1)
        acc = None
        scores = [_dot_nt(q_ref[:, e * HEAD_PAD:(e + 1) * HEAD_PAD], k_ref[0:nk, e * HEAD_PAD:(e + 1) * HEAD_PAD])
                  for e in range(2)]
        for e, (v_ref, sum_lane) in enumerate(((va_ref, ATT_SUM_LANE_A), (vb_ref, ATT_SUM_LANE_B))):
            s = scores[e]
            m = jnp.max(s, axis=-1, keepdims=True)
            p = jnp.exp2((s - m) * (MLA_SCALE * LOG2E))
            o = _dot(p.astype(BF16), v_ref[0:nk, :])
            l = jnp.sum(jnp.where(lane == sum_lane, o, 0.0), axis=-1, keepdims=True)
            o = jnp.where((lane // MLA_V) == e, o / l, 0.0)
            acc = o if acc is None else acc + o
        o_ref[...] = acc.astype(BF16)

    @pl.when(i < nct)
    def _():
        attend(lc)

    @pl.when(i >= nct)
    def _():
        attend(k_ref.shape[0])


def _attention(q, k, va, vb, tq, lc):
    bsz, t, _ = q.shape
    npair = MLA_HEADS // 2
    return pl.pallas_call(
        functools.partial(_attn_kernel, lc=lc, nct=lc // tq),
        grid=(bsz, npair, t // tq),
        in_specs=[pl.BlockSpec((None, tq, 2 * HEAD_PAD), lambda b, p, i: (b, i, p)),
                  pl.BlockSpec((None, t, 2 * HEAD_PAD), lambda b, p, i: (b, 0, p)),
                  pl.BlockSpec((None, t, 2 * MLA_V), lambda b, p, i: (b, 0, p)),
                  pl.BlockSpec((None, t, 2 * MLA_V), lambda b, p, i: (b, 0, p))],
        out_specs=pl.BlockSpec((None, tq, 2 * MLA_V), lambda b, p, i: (b, i, p)),
        out_shape=jax.ShapeDtypeStruct((bsz, t, MLA_HEADS * MLA_V), BF16),
        compiler_params=_cparams(("parallel", "parallel", "parallel")),
        name="mla_attention",
    )(q, k, va, vb)


def _chunk_of_step(s, n, ncc):
    r = s - n
    rev = jnp.where(r < ncc, ncc - 1 - r, n - 1 - (r - ncc))
    return jnp.where(s < n, s, rev)


def _scan_masks(rev):
    ti = lax.broadcasted_iota(jnp.int32, (CHUNK, CHUNK), 0)
    si = lax.broadcasted_iota(jnp.int32, (CHUNK, CHUNK), 1)
    eye = ti == si
    if rev:
        cum_m = (ti >= si).astype(BF16)
        mask = si >= ti
        last = si[0:1, :] == 0
    else:
        cum_m = (ti <= si).astype(BF16)
        mask = si <= ti
        last = si[0:1, :] == CHUNK - 1
    return cum_m, mask, eye, last


def _to_col(row, eye):
    return jnp.sum(jnp.where(eye, row, 0.0), axis=1, keepdims=True)


def _pick_last(row, last):
    return jnp.sum(jnp.where(last, row, 0.0), axis=1, keepdims=True)


def _ssd_kernel(xs_ref, bm_ref, cm_ref, z_ref, dt_ref, dtb_ref, alog_ref, dskip_ref, nw_ref,
                o_ref, ybuf, ht, *, n, ncc):
    s = pl.program_id(1)
    c = _chunk_of_step(s, n, ncc)
    hpg = SSM_HEADS // SSM_GROUPS
    gw = hpg * SSM_HEADDIM

    @pl.when(jnp.logical_or(s == 0, s == n))
    def _():
        ht[...] = jnp.zeros_like(ht)

    def chunk(rev, j):
        d = 1 if rev else 0
        rs = slice(j * CHUNK, (j + 1) * CHUNK)
        cum_m, mask, eye, last = _scan_masks(rev)
        rows = slice(d * SSM_HEADS, (d + 1) * SSM_HEADS)
        dt_rows = jnp.transpose(dt_ref[rs, :])
        dt = _softplus(dt_rows[rows, :] + dtb_ref[rows, :])
        da = dt * (-jnp.exp(alog_ref[rows, :]))
        cum = _dot_f32_rows(da, cum_m)
        head_of_lane = lax.broadcasted_iota(jnp.int32, (CHUNK, gw), 1) // SSM_HEADDIM
        xs = xs_ref[rs, :]
        cols = [_to_col(cum[h:h + 1, :], eye) for h in range(SSM_HEADS)]
        dtcols = [_to_col(dt[h:h + 1, :], eye) for h in range(SSM_HEADS)]
        tots = [_pick_last(cum[h:h + 1, :], last) for h in range(SSM_HEADS)]
        bmbs, cmbs, cbs, xgs, xgbs, htgs, inters = [], [], [], [], [], [], []
        for g in range(SSM_GROUPS):
            bmbs.append(bm_ref[rs, g * SSM_STATE:(g + 1) * SSM_STATE].astype(BF16))
            cmbs.append(cm_ref[rs, g * SSM_STATE:(g + 1) * SSM_STATE].astype(BF16))
            cbs.append(_dot_nt(cmbs[g], bmbs[g]))
            xgs.append(xs[:, g * gw:(g + 1) * gw])
            xgbs.append(xgs[g].astype(BF16))
            htgs.append(ht[g])
            inters.append(_dot(cmbs[g], htgs[g].astype(BF16)))
        ys = []
        for g in range(SSM_GROUPS):
            yg = jnp.zeros((CHUNK, gw), F32)
            eg = jnp.zeros((CHUNK, gw), F32)
            teg = jnp.zeros((CHUNK, gw), F32)
            dg = jnp.zeros((1, gw), F32)
            for r in range(hpg):
                h = g * hpg + r
                row = cum[h:h + 1, :]
                dtrow = dt[h:h + 1, :]
                col, dtcol, tot = cols[h], dtcols[h], tots[h]
                sel = head_of_lane == r
                eg = jnp.where(sel, jnp.exp(col), eg)
                teg = jnp.where(sel, jnp.exp(tot - col) * dtcol, teg)
                dg = jnp.where(sel[0:1, :], jnp.exp(tot), dg)
                decay = jnp.exp(jnp.where(mask, col - row, -jnp.inf))
                w = cbs[g] * decay * dtrow
                yh = _dot(w.astype(BF16), xgbs[g])
                yg = jnp.where(sel, yh, yg)
            ht[g] = htgs[g] * dg + _dot_tn(bmbs[g], (xgs[g] * teg).astype(BF16))
            ys.append(yg + inters[g] * eg)
        return jnp.concatenate(ys, axis=1), xs

    @pl.when(s < n)
    def _():
        for j in range(SCAN_SUB):
            y, xs = chunk(False, j)
            ybuf[c * SCAN_SUB + j] = y + dskip_ref[...] * xs

    @pl.when(s >= n)
    def _():
        for j in reversed(range(SCAN_SUB)):
            rs = slice(j * CHUNK, (j + 1) * CHUNK)
            y, _ = chunk(True, j)
            y = y + ybuf[c * SCAN_SUB + j]
            z = z_ref[rs, :]
            gated = y * (z * _sigmoid(z))
            outs = []
            for g in range(SSM_GROUPS):
                gg = gated[:, g * gw:(g + 1) * gw]
                outs.append(gg * lax.rsqrt(jnp.mean(gg * gg, axis=-1, keepdims=True) + EPS))
            o_ref[rs, :] = (jnp.concatenate(outs, axis=1) * nw_ref[...]).astype(BF16)


def _ssd(cv, p, dt_bias, a_log, d_skip, norm_w, lc):
    bsz, t, _ = cv.shape
    sb = SCAN_SUB * CHUNK
    assert t % sb == 0 and lc % sb == 0
    n = t // sb
    ncc = lc // sb
    cidx = lambda s: _chunk_of_step(s, n, ncc)
    cidx_out = lambda s: _chunk_of_step(jnp.maximum(s, n), n, ncc)
    nst = SSM_GROUPS * SSM_STATE
    small = lambda shape: pl.BlockSpec(shape, lambda b, s: (0, 0))
    return pl.pallas_call(
        functools.partial(_ssd_kernel, n=n, ncc=ncc),
        grid=(bsz, 2 * n),
        in_specs=[pl.BlockSpec((None, sb, SSM_DINNER), lambda b, s: (b, cidx(s), 0)),
                  pl.BlockSpec((None, sb, nst), lambda b, s: (b, cidx(s), SSM_DINNER // nst)),
                  pl.BlockSpec((None, sb, nst), lambda b, s: (b, cidx(s), SSM_DINNER // nst + 1)),
                  pl.BlockSpec((None, sb, SSM_DINNER), lambda b, s: (b, cidx_out(s), EV_Z // SSM_DINNER)),
                  pl.BlockSpec((None, sb, LANES), lambda b, s: (b, cidx(s), EV_DT // LANES)),
                  small((2 * SSM_HEADS, 1)), small((2 * SSM_HEADS, 1)),
                  small((1, SSM_DINNER)), small((1, SSM_DINNER))],
        out_specs=pl.BlockSpec((None, sb, SSM_DINNER), lambda b, s: (b, cidx_out(s), 0)),
        out_shape=jax.ShapeDtypeStruct((bsz, t, SSM_DINNER), BF16),
        scratch_shapes=[pltpu.VMEM((n * SCAN_SUB, CHUNK, SSM_DINNER), F32),
                        pltpu.VMEM((SSM_GROUPS, SSM_STATE, SSM_DINNER // SSM_GROUPS), F32)],
        compiler_params=_cparams(("parallel", "arbitrary")),
        name="ssd_scan",
    )(cv, cv, cv, p, p, dt_bias.reshape(-1, 1), a_log.reshape(-1, 1),
      jnp.repeat(d_skip, SSM_HEADDIM).reshape(1, -1), norm_w.reshape(1, -1))


MLSTM_XR = 16


def _mlstm_kernel(k_ref, qt_ref, vt_ref, o_ref, g_ref, ib_ref, fb_ref, nw_ref,
                  out_ref, hbuf, cst, mst, *, n, ncc):
    s = pl.program_id(1)
    c = _chunk_of_step(s, n, ncc)
    nh, dk, dv = MLSTM_HEADS, MLSTM_DQK, MLSTM_DV
    pw = 2 * dk

    @pl.when(jnp.logical_or(s == 0, s == n))
    def _():
        cst[...] = jnp.zeros_like(cst)
        mst[...] = jnp.zeros_like(mst)

    def chunk(rev, j):
        d = 1 if rev else 0
        rs = slice(j * CHUNK, (j + 1) * CHUNK)
        si = lax.broadcasted_iota(jnp.int32, (CHUNK, CHUNK), 0)
        ti = lax.broadcasted_iota(jnp.int32, (CHUNK, CHUNK), 1)
        seen = (si >= ti) if rev else (si <= ti)
        cum_m = seen.astype(BF16)
        g_rows = jnp.transpose(g_ref[rs, :])
        ipre = g_rows[d * nh:(d + 1) * nh, :] + ib_ref[d * nh:(d + 1) * nh, :]
        fpre = g_rows[(2 + d) * nh:(3 + d) * nh, :] + fb_ref[d * nh:(d + 1) * nh, :]
        bcum = _dot_f32_rows(-_softplus(-fpre), cum_m)
        lane = lax.broadcasted_iota(jnp.int32, (1, CHUNK), 1)
        btot = jnp.sum(jnp.where(lane == (0 if rev else CHUNK - 1), bcum, 0.0), axis=1, keepdims=True)
        cs = ipre - bcum
        wend = btot + cs
        m_in = mst[:, 0:1]
        m_new = jnp.maximum(btot + m_in, jnp.max(wend, axis=1, keepdims=True))
        keep = jnp.exp(btot + m_in - m_new)
        wt = jnp.exp(wend - m_new)
        mst[...] = jnp.broadcast_to(m_new, mst.shape)

        row_half = lax.broadcasted_iota(jnp.int32, (pw, CHUNK), 0) // dk
        lane_half = lax.broadcasted_iota(jnp.int32, (CHUNK, pw), 1) // dk
        ones_rows = (lax.broadcasted_iota(jnp.int32, (MLSTM_XR, CHUNK), 0) == 0).astype(BF16)
        zblk = jnp.zeros((CHUNK, CHUNK), BF16)
        outs = []
        sts, inters, vtps = [], [], []
        for p in range(nh // 2):
            kp = k_ref[rs, p * pw:(p + 1) * pw] * (dk ** -0.5)
            qtp = qt_ref[j, p * pw:(p + 1) * pw, :]
            qbd = jnp.concatenate([jnp.where(row_half == 0, qtp, 0.0), jnp.where(row_half == 1, qtp, 0.0)],
                                  axis=1).astype(BF16)
            sts.append(_dot(kp.astype(BF16), qbd))
            cp = cst[p]
            inters.append(_dot(cp.astype(BF16), qbd))
            vtp = jnp.concatenate(
                [jnp.concatenate([vt_ref[j, (2 * p + e) * dv:(2 * p + e + 1) * dv, :], ones_rows], axis=0)
                 for e in range(2)], axis=1)
            vtps.append(vtp.astype(BF16))
            wtp = jnp.concatenate([wt[2 * p:2 * p + 1, :], wt[2 * p + 1:2 * p + 2, :]], axis=1)
            kbd = jnp.concatenate([jnp.where(lane_half == 0, kp, 0.0), jnp.where(lane_half == 1, kp, 0.0)],
                                  axis=0).astype(BF16)
            keepp = jnp.where(lane_half[0:1, :] == 0, keep[2 * p:2 * p + 1, :], keep[2 * p + 1:2 * p + 2, :])
            cst[p] = cp * keepp + _dot((vtp * wtp).astype(BF16), kbd)
        for p in range(nh // 2):
            st = sts[p]
            ats, keepqs, mrows = [], [], []
            for e in range(2):
                h = 2 * p + e
                cs_st = jnp.transpose(jnp.broadcast_to(cs[h:h + 1, :], (CHUNK, CHUNK)))
                ex = jnp.where(seen, cs_st, -jnp.inf)
                mrow = jnp.maximum(m_in[h:h + 1, :], jnp.max(ex, axis=0, keepdims=True))
                ats.append((st[:, e * CHUNK:(e + 1) * CHUNK] * jnp.exp(ex - mrow)).astype(BF16))
                keepqs.append(jnp.exp(m_in[h:h + 1, :] - mrow))
                mrows.append(mrow)
            abd = jnp.concatenate([jnp.concatenate([ats[0], zblk], axis=1),
                                   jnp.concatenate([zblk, ats[1]], axis=1)], axis=0)
            nd = _dot(vtps[p], abd) + inters[p] * jnp.concatenate(keepqs, axis=1)
            for e in range(2):
                h = 2 * p + e
                num = nd[0:dv, e * CHUNK:(e + 1) * CHUNK]
                den = nd[dv:dv + 1, e * CHUNK:(e + 1) * CHUNK]
                mt = bcum[h:h + 1, :] + mrows[e]
                outs.append(num * (1.0 / jnp.maximum(jnp.abs(den), jnp.exp(-mt))))
        return jnp.concatenate(outs, axis=0)

    @pl.when(s < n)
    def _():
        for j in range(SCAN_SUB):
            hbuf[c * SCAN_SUB + j] = chunk(False, j)

    @pl.when(s >= n)
    def _():
        for j in reversed(range(SCAN_SUB)):
            rs = slice(j * CHUNK, (j + 1) * CHUNK)
            ht = chunk(True, j) + hbuf[c * SCAN_SUB + j]
            outs = []
            for h in range(nh):
                hh = ht[h * dv:(h + 1) * dv, :]
                hn = hh * lax.rsqrt(jnp.mean(hh * hh, axis=0, keepdims=True) + EPS)
                outs.append(jnp.transpose(hn))
            hn = jnp.concatenate(outs, axis=1) * nw_ref[...]
            out_ref[rs, :] = (_sigmoid(o_ref[rs, :]) * hn).astype(BF16)


def _mlstm(cv, q_t, v_t, p, i_bias, f_bias, head_norm_w, lc):
    bsz, t, _ = cv.shape
    sb = SCAN_SUB * CHUNK
    assert t % sb == 0 and lc % sb == 0
    n = t // sb
    ncc = lc // sb
    cidx = lambda s: _chunk_of_step(s, n, ncc)
    cidx_out = lambda s: _chunk_of_step(jnp.maximum(s, n), n, ncc)
    nh = MLSTM_HEADS
    small = lambda shape: pl.BlockSpec(shape, lambda b, s: (0, 0))
    return pl.pallas_call(
        functools.partial(_mlstm_kernel, n=n, ncc=ncc),
        grid=(bsz, 2 * n),
        in_specs=[pl.BlockSpec((None, sb, MLSTM_QK), lambda b, s: (b, cidx(s), 1)),
                  pl.BlockSpec((None, SCAN_SUB, MLSTM_QK, CHUNK), lambda b, s: (b, cidx(s), 0, 0)),
                  pl.BlockSpec((None, SCAN_SUB, MLSTM_VW, CHUNK), lambda b, s: (b, cidx(s), 0, 0)),
                  pl.BlockSpec((None, sb, MLSTM_VW), lambda b, s: (b, cidx_out(s), OD_O // MLSTM_VW)),
                  pl.BlockSpec((None, sb, LANES), lambda b, s: (b, cidx(s), OD_G // LANES)),
                  small((2 * nh, 1)), small((2 * nh, 1)), small((1, MLSTM_VW))],
        out_specs=pl.BlockSpec((None, sb, MLSTM_VW), lambda b, s: (b, cidx_out(s), 0)),
        out_shape=jax.ShapeDtypeStruct((bsz, t, MLSTM_VW), BF16),
        scratch_shapes=[pltpu.VMEM((n * SCAN_SUB, MLSTM_VW, CHUNK), F32),
                        pltpu.VMEM((nh // 2, MLSTM_DV + MLSTM_XR, 2 * MLSTM_DQK), F32),
                        pltpu.VMEM((nh, LANES), F32)],
        compiler_params=_cparams(("parallel", "arbitrary")),
        name="mlstm_scan",
    )(cv, q_t, v_t, p, p, i_bias.reshape(-1, 1), f_bias.reshape(-1, 1), head_norm_w.reshape(1, -1))


def _mix_mlp_kernel(*refs, nin, tf):
    ms = refs[:nin]
    ws = refs[nin:2 * nin]
    x_ref, g1_ref, nw_ref, sh_ref, sc_ref, g2_ref, w1_ref, w2_ref, fw_ref, o_ref = refs[2 * nin:]
    mix = _dot(ms[0][...], ws[0][...])
    for m_ref, w_ref in zip(ms[1:], ws[1:]):
        mix = mix + _dot(m_ref[...], w_ref[...])
    x = x_ref[...] + g1_ref[...] * mix
    h = (_rms(x, nw_ref[...]) * (1.0 + sc_ref[...]) + sh_ref[...]).astype(BF16)
    acc = jnp.zeros(x.shape, F32)
    for f in range(0, w1_ref.shape[1], tf):
        u = jnp.maximum(_dot(h, w1_ref[:, f:f + tf]), 0.0)
        acc = acc + _dot((u * u).astype(BF16), w2_ref[f:f + tf, :])
    x = x + g2_ref[...] * acc
    o_ref[...] = x if fw_ref is None else _rms(x, fw_ref[...])


def _mix_mlp(ms, ws, xs, nw, mod, w1, w2, tm, nct, final_w=None):
    bsz, t, d = xs.shape
    dff = w1.shape[1]
    nin = len(ms)
    skip = nct if final_w is not None else 0
    rows = lambda width: pl.BlockSpec((None, tm, width), lambda b, i: (b, i + skip, 0))
    full = lambda shape: pl.BlockSpec(shape, lambda b, i: (0, 0))
    mspec = lambda chunk: _mod_spec(d, chunk, nct - skip)
    args = [*ms, *ws, xs, mod, nw.reshape(1, d), mod, mod, mod, w1, w2]
    in_specs = ([rows(m.shape[2]) for m in ms] + [full(w.shape) for w in ws]
                + [rows(d), mspec(2), full((1, d)), mspec(3), mspec(4), mspec(5), full((d, dff)), full((dff, d))])
    kern = functools.partial(_mix_mlp_kernel, nin=nin, tf=1024)
    if final_w is not None:
        args.append(final_w.reshape(1, d))
        in_specs.append(full((1, d)))
    else:
        kern = functools.partial(_no_final, kern, len(args))
    return pl.pallas_call(
        kern,
        grid=(bsz, t // tm - skip),
        in_specs=in_specs,
        out_specs=pl.BlockSpec((None, tm, d), lambda b, i: (b, i, 0)),
        out_shape=jax.ShapeDtypeStruct((bsz, t - skip * tm, d), F32),
        compiler_params=_cparams(("parallel", "parallel")),
        name="outproj_mlp_residual",
    )(*args)


def _no_final(kern, nargs, *refs):
    kern(*refs[:nargs], None, *refs[nargs:])


def _rope_swap(w):
    j = np.arange(MLA_ROPE)
    first = (j % 16) < 8
    src = np.where(first, j + 8, j - 8)
    sign = np.where(first, -1.0, 1.0).astype(np.float32)
    return w[..., src] * sign


def _rope_tables(lc, l):
    n_rows = l // GRID_W
    row = jnp.repeat(jnp.arange(n_rows), GRID_W).astype(F32)
    col = jnp.tile(jnp.arange(GRID_W), n_rows).astype(F32)
    half = MLA_ROPE // 2
    inv = 1.0 / (ROPE_THETA ** (jnp.arange(0, half, 2, dtype=F32) / half))
    ar = row[:, None] * inv
    ac = col[:, None] * inv
    cos_l = jnp.concatenate([jnp.ones((l, MLA_NOPE), F32), jnp.cos(ar), jnp.cos(ar), jnp.cos(ac), jnp.cos(ac),
                             jnp.ones((l, HEAD_PAD - MLA_NOPE - MLA_ROPE), F32)], axis=1)
    sin_l = jnp.concatenate([jnp.zeros((l, MLA_NOPE), F32), jnp.sin(ar), jnp.sin(ar), jnp.sin(ac), jnp.sin(ac),
                             jnp.zeros((l, HEAD_PAD - MLA_NOPE - MLA_ROPE), F32)], axis=1)
    cos_t = jnp.concatenate([jnp.ones((lc, HEAD_PAD), F32), cos_l], axis=0)
    sin_t = jnp.concatenate([jnp.zeros((lc, HEAD_PAD), F32), sin_l], axis=0)
    return cos_t, sin_t


def _even_weights(w_in, w_uq, w_ukv, w_out):
    d = w_in.shape[0]
    o = np.cumsum([0, Q_LORA, KV_LORA, MLA_ROPE, SSM_DINNER, SSM_XBC, 2 * SSM_HEADS])
    cq, ckv, kr, z, xbc, dt = (w_in[:, o[i]:o[i + 1]] for i in range(6))
    zc = lambda n: jnp.zeros((d, n), F32)
    w_in_p = jnp.concatenate(
        [cq, ckv, z, xbc,
         zc(MLA_NOPE), kr, zc(HEAD_PAD - MLA_NOPE - MLA_ROPE),
         zc(MLA_NOPE), _rope_swap(kr), zc(HEAD_PAD - MLA_NOPE - MLA_ROPE),
         dt, zc(LANES - 2 * SSM_HEADS)], axis=1).astype(BF16)
    assert w_in_p.shape[1] == EV_N
    uq = w_uq.reshape(Q_LORA, MLA_HEADS, MLA_NOPE + MLA_ROPE)
    q_nope, q_rope = uq[..., :MLA_NOPE], uq[..., MLA_NOPE:]
    zq = lambda n: jnp.zeros((Q_LORA, MLA_HEADS, n), F32)
    npad = HEAD_PAD - MLA_NOPE - MLA_ROPE
    wq = jnp.concatenate([q_nope, q_rope, zq(npad)], axis=-1).reshape(Q_LORA, -1).astype(BF16)
    wqs = jnp.concatenate([zq(MLA_NOPE), _rope_swap(q_rope), zq(npad)], axis=-1).reshape(Q_LORA, -1).astype(BF16)
    ukv = w_ukv.reshape(KV_LORA, MLA_HEADS, MLA_NOPE + MLA_V)
    k_nope, vw = ukv[..., :MLA_NOPE], ukv[..., MLA_NOPE:]
    wk = jnp.concatenate([k_nope, jnp.zeros((KV_LORA, MLA_HEADS, HEAD_PAD - MLA_NOPE), F32)], axis=-1)
    wk = wk.reshape(KV_LORA, -1).astype(BF16)
    vpair = vw.reshape(KV_LORA, MLA_HEADS // 2, 2, MLA_V)
    zv = jnp.zeros((KV_LORA, MLA_HEADS // 2, MLA_V), F32)
    wva = jnp.concatenate([vpair[:, :, 0], zv], axis=-1).reshape(KV_LORA, -1).astype(BF16)
    wvb = jnp.concatenate([zv, vpair[:, :, 1]], axis=-1).reshape(KV_LORA, -1).astype(BF16)
    hv = MLA_HEADS * MLA_V
    return w_in_p, wq, wqs, wk, wva, wvb, w_out[:hv].astype(BF16), w_out[hv:].astype(BF16)


def _odd_weights(w_in):
    d = w_in.shape[0]
    ngate = 4 * MLSTM_HEADS
    w = jnp.concatenate([w_in, jnp.zeros((d, OD_N - OD_G - ngate), F32)], axis=1).astype(BF16)
    assert w.shape[1] == OD_N
    return w


def kernel(x, c, ctx, c_ctx, ada_w, ada_b, norm_mix_w, norm_mlp_w, mlp_w1, mlp_w2, ev_w_in, ev_q_norm_w, ev_w_uq, ev_kv_norm_w, ev_w_ukv, ev_conv_w, ev_conv_b, ev_dt_bias, ev_a_log, ev_d_skip, ev_ssm_norm_w, ev_w_out, od_w_in, od_conv_w, od_conv_b, od_i_bias, od_f_bias, od_head_norm_w, od_w_out, final_norm_w):
    bsz, l, d = x.shape
    lc = ctx.shape[1]
    depth = ada_w.shape[0]
    tm = 256 if lc % 256 == 0 else CHUNK
    assert lc % tm == 0 and l % tm == 0 and l % GRID_W == 0
    nct = lc // tm

    rows = -(-(bsz + 1) // SUBLANES) * SUBLANES
    svec = jnp.concatenate([c, c_ctx[None], jnp.zeros((rows - bsz - 1, d), F32)], axis=0)
    mods = _ada(svec, ada_w, ada_b)
    mod_ctx = jnp.broadcast_to(mods[:, bsz][:, None], (depth, bsz, 6 * d))
    mod_all = jnp.stack([mod_ctx, mods[:, :bsz]], axis=2)[:, :, :, None, :]

    cos_t, sin_t = _rope_tables(lc, l)
    xs = jnp.concatenate([ctx, x], axis=1)

    for layer in range(depth):
        mod = mod_all[layer]
        if layer % 2 == 0:
            e = layer // 2
            w_in_p, wq, wqs, wk, wva, wvb, wo_a, wo_s = _even_weights(ev_w_in[e], ev_w_uq[e], ev_w_ukv[e], ev_w_out[e])
            (p,) = _inproj(xs, norm_mix_w[layer], mod, w_in_p, tm, nct)
            q, k, va, vb = _mla_prep(p, cos_t, sin_t, ev_q_norm_w[e], ev_kv_norm_w[e], wq, wqs, wk, wva, wvb, tm)
            o_attn = _attention(q, k, va, vb, tm, lc)
            cv = _conv(p, EV_XBC, ev_conv_w[e], ev_conv_b[e], lc, l)
            s_ssd = _ssd(cv, p, ev_dt_bias[e], ev_a_log[e], ev_d_skip[e], ev_ssm_norm_w[e], lc)
            mix, mix_w = [o_attn, s_ssd], [wo_a, wo_s]
        else:
            o = layer // 2
            p, v_t = _inproj(xs, norm_mix_w[layer], mod, _odd_weights(od_w_in[o]), tm, nct,
                             tcols=(OD_V, MLSTM_VW))
            cv = _conv(p, OD_QK, od_conv_w[o], od_conv_b[o], lc, l)
            q_t = jnp.swapaxes(cv[:, :, :MLSTM_QK].astype(BF16).reshape(bsz, -1, CHUNK, MLSTM_QK), 2, 3)
            m = _mlstm(cv, q_t, v_t, p, od_i_bias[o], od_f_bias[o], od_head_norm_w[o], lc)
            mix, mix_w = [m], [od_w_out[o].astype(BF16)]
        xs = _mix_mlp(mix, mix_w, xs, norm_mlp_w[layer], mod, mlp_w1[layer].astype(BF16),
                      mlp_w2[layer].astype(BF16), tm, nct, final_w=final_norm_w if layer == depth - 1 else None)
    return xs
```
